```python
import math
import jax, jax.numpy as jnp
from jax import lax
import numpy as np

D_MODEL = 1024
BATCH = 16
SEQ = 2048
DEPTH = 4

N_MIXERS = 4
N_LAYERS_CONV = (DEPTH + 3) // N_MIXERS
N_LAYERS_RWKV = (DEPTH + 2) // N_MIXERS
N_LAYERS_SGU = (DEPTH + 1) // N_MIXERS
N_LAYERS_RET = DEPTH // N_MIXERS

N_MEM = 256
XA_HEADS = 4
XA_HEAD_DIM = D_MODEL // XA_HEADS

CONV_WIDTH = 31

RWKV_HEAD_DIM = 64
RWKV_HEADS = D_MODEL // RWKV_HEAD_DIM
RWKV_DECAY_LORA = 64
RWKV_AAA_LORA = 64
RWKV_GATE_LORA = 160
RWKV_GN_EPS = RWKV_HEAD_DIM * 1e-5

SGU_CHUNK = 128
SGU_WIDTH = 2 * D_MODEL
SGU_GROUPS = 8
SGU_GROUP_DIM = SGU_WIDTH // SGU_GROUPS

RET_HEADS = 4
RET_QK_DIM = D_MODEL // RET_HEADS
RET_V_DIM = 2 * RET_QK_DIM
RET_CHUNK = 128

D_FF = 256 * math.ceil(8 * D_MODEL / 3 / 256)
NORM_EPS = 1e-6
LN_EPS = 1e-5

kernel_name = 'hybrid_conv_rwkv7_sgu_retention_trunk'


def rmsnorm(x, g, eps=NORM_EPS):
    x32 = x.astype(jnp.float32)
    y = x32 * lax.rsqrt(jnp.mean(x32 * x32, axis=-1, keepdims=True) + eps)
    return (y * g.astype(jnp.float32)).astype(x.dtype)


def layernorm(x, g, b, eps=LN_EPS):
    x32 = x.astype(jnp.float32)
    mu = jnp.mean(x32, axis=-1, keepdims=True)
    var = jnp.mean(jnp.square(x32 - mu), axis=-1, keepdims=True)
    y = (x32 - mu) * lax.rsqrt(var + eps)
    return (y * g.astype(jnp.float32) + b.astype(jnp.float32)).astype(x.dtype)


def conv_module(h, w_in, b_in, dw, dw_b, ln_g, ln_b, w_out, b_out):
    z = h @ w_in + b_in
    a, gate = jnp.split(z, 2, axis=-1)
    z = a * jax.nn.sigmoid(gate)
    z = lax.conv_general_dilated(
        z, dw[:, None, :].astype(z.dtype), window_strides=(1,),
        padding=[(CONV_WIDTH - 1, 0)],
        dimension_numbers=('NWC', 'WIO', 'NWC'),
        feature_group_count=D_MODEL) + dw_b
    z = jax.nn.silu(layernorm(z, ln_g, ln_b))
    return z @ w_out + b_out


def rwkv7_time_mix(h, mix, w_rkv, w0, w1, w2, a0, a1, a2, g1, g2, k_k, k_a, r_k, gn_g, gn_b, w_o):
    B, T, D = h.shape
    H, N = RWKV_HEADS, RWKV_HEAD_DIM
    h_prev = jnp.pad(h, ((0, 0), (1, 0), (0, 0)))[:, :T]
    xs = h[None] + (h_prev - h)[None] * mix[:, None, None, :]
    r = xs[0] @ w_rkv[0]
    k = xs[2] @ w_rkv[1]
    v = xs[3] @ w_rkv[2]
    w = -jax.nn.softplus(-(w0 + jnp.tanh(xs[1] @ w1) @ w2)) - 0.5
    decay = jnp.exp(-jnp.exp(w.astype(jnp.float32)))
    a = jax.nn.sigmoid(a0 + (xs[4] @ a1) @ a2)
    g = jax.nn.sigmoid(xs[5] @ g1) @ g2

    def heads(t):
        return t.astype(jnp.float32).reshape(B, T, H, N)

    kk = heads(k * k_k)
    kk = kk / jnp.maximum(jnp.sqrt(jnp.sum(kk * kk, axis=-1, keepdims=True)), 1e-12)
    k = k * (1.0 + (a - 1.0) * k_a)
    r_h, k_h, v_h, a_h, w_h = heads(r), heads(k), heads(v), heads(a), heads(decay)

    def step(S, inp):
        r_t, w_t, k_t, v_t, a_t, b_t = inp
        sa = jnp.einsum('bhij,bhj->bhi', S, a_t)
        S = S * w_t[:, :, None, :] + sa[..., None] * b_t[:, :, None, :] + v_t[..., None] * k_t[:, :, None, :]
        return S, jnp.einsum('bhij,bhj->bhi', S, r_t)

    def tmajor(t):
        return jnp.swapaxes(t, 0, 1)

    S0 = jnp.zeros((B, H, N, N), jnp.float32)
    _, y = lax.scan(step, S0, (tmajor(r_h), tmajor(w_h), tmajor(k_h), tmajor(v_h),
                               tmajor(-kk), tmajor(kk * a_h)))
    y = tmajor(y)
    y = layernorm(y, gn_g.reshape(H, N), gn_b.reshape(H, N), RWKV_GN_EPS)
    y = y + jnp.sum(r_h * k_h * r_k, axis=-1, keepdims=True) * v_h
    return (y.reshape(B, T, D) * g) @ w_o


def spatial_gating_mixer(h, w_in, b_in, ln_g, ln_b, ws, bs, w_out):
    B, T, _ = h.shape
    z = jax.nn.gelu(h @ w_in + b_in, approximate=False)
    u, v = jnp.split(z, 2, axis=-1)
    v = layernorm(v, ln_g, ln_b)
    n_chunks = T // SGU_CHUNK
    v = v.reshape(B, n_chunks, SGU_CHUNK, SGU_GROUPS, SGU_GROUP_DIM)
    causal = jnp.tril(jnp.ones((SGU_CHUNK, SGU_CHUNK), dtype=bool))
    ws_c = jnp.where(causal[None], ws, jnp.zeros((), ws.dtype))
    v = jnp.einsum('gts,bcsge->bctge', ws_c, v) + jnp.transpose(bs)[None, None, :, :, None]
    return (u * v.reshape(B, T, SGU_WIDTH)) @ w_out


def rotate_every_two(x):
    x1 = x[..., ::2]
    x2 = x[..., 1::2]
    return jnp.stack((-x2, x1), axis=-1).reshape(x.shape)


def retention_mixer(h, w_in, w_out):
    B, T, D = h.shape
    H, dk, dv, C = RET_HEADS, RET_QK_DIM, RET_V_DIM, RET_CHUNK
    q, k, v, g = jnp.split(h @ w_in, [D, 2 * D, 4 * D], axis=-1)
    q = q.reshape(B, T, H, dk)
    k = k.reshape(B, T, H, dk) * (dk ** -0.5)
    v = v.reshape(B, T, H, dv).astype(jnp.float32)
    angle = jnp.repeat(1.0 / (10000.0 ** jnp.linspace(0.0, 1.0, dk // 2, dtype=jnp.float32)), 2)
    theta = jnp.arange(T, dtype=jnp.float32)[:, None] * angle[None]
    sin, cos = jnp.sin(theta)[:, None, :], jnp.cos(theta)[:, None, :]
    q = q * cos + rotate_every_two(q) * sin
    k = k * cos + rotate_every_two(k) * sin

    log_gamma = jnp.log(1.0 - 2.0 ** (-5.0 - jnp.arange(H, dtype=jnp.float32)))
    n_chunks = T // C

    def chunked(t):
        return t.reshape(B, n_chunks, C, H, t.shape[-1]).transpose(0, 3, 1, 2, 4)

    q, k, v = chunked(q), chunked(k), chunked(v)
    idx = jnp.arange(C, dtype=jnp.float32)
    diff = idx[:, None] - idx[None, :]
    decay_mask = jnp.where(diff[None] >= 0,
                           jnp.exp(log_gamma[:, None, None] * jnp.maximum(diff, 0.0)[None]), 0.0)
    scores = jnp.einsum('bhncd,bhnsd->bhncs', q, k) * decay_mask[:, None]
    inner = jnp.einsum('bhncs,bhnse->bhnce', scores, v)

    q_decay = jnp.exp(log_gamma[:, None] * (idx + 1.0))
    k_decay = jnp.exp(log_gamma[:, None] * (C - 1.0 - idx))
    chunk_decay = jnp.exp(log_gamma * C)
    qs = q * q_decay[:, None, :, None]
    ks = k * k_decay[:, None, :, None]

    def step(R, inp):
        qc, kc, vc = inp
        cross = jnp.einsum('bhcd,bhde->bhce', qc, R)
        R = R * chunk_decay[None, :, None, None] + jnp.einsum('bhcd,bhce->bhde', kc, vc)
        return R, cross

    def nmajor(t):
        return jnp.moveaxis(t, 2, 0)

    R0 = jnp.zeros((B, H, dk, dv), jnp.float32)
    _, cross = lax.scan(step, R0, (nmajor(qs), nmajor(ks), nmajor(v)))
    o = inner + jnp.moveaxis(cross, 0, 2)
    o = o.transpose(0, 2, 3, 1, 4).reshape(B, T, H, dv)
    o = o * lax.rsqrt(jnp.mean(o * o, axis=-1, keepdims=True) + NORM_EPS)
    return (jax.nn.silu(g) * o.reshape(B, T, H * dv)) @ w_out


def memory_cross_attention(h, mem_n, wq, wkv, wo):
    B, T, D = h.shape
    M = mem_n.shape[1]
    q = (h @ wq).reshape(B, T, XA_HEADS, XA_HEAD_DIM)
    k, v = jnp.split(mem_n @ wkv, 2, axis=-1)
    k = k.reshape(B, M, XA_HEADS, XA_HEAD_DIM)
    v = v.reshape(B, M, XA_HEADS, XA_HEAD_DIM)
    s = jnp.einsum('bthd,bmhd->bhtm', q, k).astype(jnp.float32) * (XA_HEAD_DIM ** -0.5)
    p = jax.nn.softmax(s, axis=-1).astype(v.dtype)
    o = jnp.einsum('bhtm,bmhd->bthd', p, v).reshape(B, T, D)
    return o @ wo


def swiglu(h, w_gate_up, w_down):
    gt, up = jnp.split(h @ w_gate_up, 2, axis=-1)
    return (jax.nn.silu(gt) * up) @ w_down


def setup_inputs(seed: int = 0) -> dict:
    key = jax.random.key(seed)
    ks = iter(jax.random.split(key, 64))
    D, F, E = D_MODEL, D_FF, SGU_WIDTH
    nA, nB, nC, nD = N_LAYERS_CONV, N_LAYERS_RWKV, N_LAYERS_SGU, N_LAYERS_RET

    def normal(shape, scale):
        return scale * jax.random.normal(next(ks), shape, jnp.float32)

    def gain(shape):
        return 1.0 + normal(shape, 0.02)

    def uniform(shape, lo, hi):
        return jax.random.uniform(next(ks), shape, jnp.float32, lo, hi)

    return {
        'x': normal((BATCH, SEQ, D), 1.0),
        'mem': normal((BATCH, N_MEM, D), 1.0),
        'mix_norm_g': gain((DEPTH, D)),
        'xattn_norm_g': gain((DEPTH, D)),
        'mem_norm_g': gain((DEPTH, D)),
        'xattn_wq': normal((DEPTH, D, D), D ** -0.5),
        'xattn_wkv': normal((DEPTH, D, 2 * D), D ** -0.5),
        'xattn_wo': normal((DEPTH, D, D), D ** -0.5),
        'ffn_norm_g': gain((DEPTH, D)),
        'ffn_w_gate_up': normal((DEPTH, D, 2 * F), D ** -0.5),
        'ffn_w_down': normal((DEPTH, F, D), F ** -0.5),
        'conv_w_in': normal((nA, D, 2 * D), D ** -0.5),
        'conv_b_in': normal((nA, 2 * D), 0.02),
        'conv_dw': normal((nA, CONV_WIDTH, D), CONV_WIDTH ** -0.5),
        'conv_dw_b': normal((nA, D), 0.02),
        'conv_ln_g': gain((nA, D)),
        'conv_ln_b': normal((nA, D), 0.02),
        'conv_w_out': normal((nA, D, D), D ** -0.5),
        'conv_b_out': normal((nA, D), 0.02),
        'rwkv_mix': uniform((nB, 6, D), 0.0, 1.0),
        'rwkv_w_rkv': normal((nB, 3, D, D), D ** -0.5),
        'rwkv_w0': uniform((nB, D), -6.0, 1.0),
        'rwkv_w1': normal((nB, D, RWKV_DECAY_LORA), D ** -0.5),
        'rwkv_w2': normal((nB, RWKV_DECAY_LORA, D), 0.1 * RWKV_DECAY_LORA ** -0.5),
        'rwkv_a0': normal((nB, D), 0.5),
        'rwkv_a1': normal((nB, D, RWKV_AAA_LORA), D ** -0.5),
        'rwkv_a2': normal((nB, RWKV_AAA_LORA, D), 0.1 * RWKV_AAA_LORA ** -0.5),
        'rwkv_g1': normal((nB, D, RWKV_GATE_LORA), D ** -0.5),
        'rwkv_g2': normal((nB, RWKV_GATE_LORA, D), RWKV_GATE_LORA ** -0.5),
        'rwkv_k_k': 0.85 + normal((nB, D), 0.02),
        'rwkv_k_a': gain((nB, D)),
        'rwkv_r_k': normal((nB, RWKV_HEADS, RWKV_HEAD_DIM), 0.1),
        'rwkv_gn_g': gain((nB, D)),
        'rwkv_gn_b': normal((nB, D), 0.02),
        'rwkv_w_o': normal((nB, D, D), D ** -0.5),
        'sgu_w_in': normal((nC, D, 2 * E), D ** -0.5),
        'sgu_b_in': normal((nC, 2 * E), 0.02),
        'sgu_ln_g': gain((nC, E)),
        'sgu_ln_b': normal((nC, E), 0.02),
        'sgu_ws': normal((nC, SGU_GROUPS, SGU_CHUNK, SGU_CHUNK), 0.5 * SGU_CHUNK ** -0.5),
        'sgu_bs': gain((nC, SGU_GROUPS, SGU_CHUNK)),
        'sgu_w_out': normal((nC, E, D), E ** -0.5),
        'ret_w_in': normal((nD, D, 6 * D), D ** -0.5),
        'ret_w_out': normal((nD, 2 * D, D), (2 * D) ** -0.5),
        'final_norm_g': gain((D,)),
    }


def reference(x, mem, mix_norm_g, xattn_norm_g, mem_norm_g, xattn_wq, xattn_wkv, xattn_wo,
              ffn_norm_g, ffn_w_gate_up, ffn_w_down,
              conv_w_in, conv_b_in, conv_dw, conv_dw_b, conv_ln_g, conv_ln_b, conv_w_out, conv_b_out,
              rwkv_mix, rwkv_w_rkv, rwkv_w0, rwkv_w1, rwkv_w2, rwkv_a0, rwkv_a1, rwkv_a2,
              rwkv_g1, rwkv_g2, rwkv_k_k, rwkv_k_a, rwkv_r_k, rwkv_gn_g, rwkv_gn_b, rwkv_w_o,
              sgu_w_in, sgu_b_in, sgu_ln_g, sgu_ln_b, sgu_ws, sgu_bs, sgu_w_out,
              ret_w_in, ret_w_out, final_norm_g):
    for i in range(DEPTH):
        m, j = i % N_MIXERS, i // N_MIXERS
        hn = rmsnorm(x, mix_norm_g[i])
        if m == 0:
            y = conv_module(hn, conv_w_in[j], conv_b_in[j], conv_dw[j], conv_dw_b[j],
                            conv_ln_g[j], conv_ln_b[j], conv_w_out[j], conv_b_out[j])
        elif m == 1:
            y = rwkv7_time_mix(hn, rwkv_mix[j], rwkv_w_rkv[j], rwkv_w0[j], rwkv_w1[j], rwkv_w2[j],
                               rwkv_a0[j], rwkv_a1[j], rwkv_a2[j], rwkv_g1[j], rwkv_g2[j],
                               rwkv_k_k[j], rwkv_k_a[j], rwkv_r_k[j], rwkv_gn_g[j], rwkv_gn_b[j],
                               rwkv_w_o[j])
        elif m == 2:
            y = spatial_gating_mixer(hn, sgu_w_in[j], sgu_b_in[j], sgu_ln_g[j], sgu_ln_b[j],
                                     sgu_ws[j], sgu_bs[j], sgu_w_out[j])
        else:
            y = retention_mixer(hn, ret_w_in[j], ret_w_out[j])
        x = x + y
        x = x + memory_cross_attention(rmsnorm(x, xattn_norm_g[i]), rmsnorm(mem, mem_norm_g[i]),
                                       xattn_wq[i], xattn_wkv[i], xattn_wo[i])
        x = x + swiglu(rmsnorm(x, ffn_norm_g[i]), ffn_w_gate_up[i], ffn_w_down[i])
    return rmsnorm(x, final_norm_g)
```

```python
import functools
import math

import jax
import jax.numpy as jnp
from jax import lax
from jax.experimental import pallas as pl
from jax.experimental.pallas import tpu as pltpu

F32 = jnp.float32
BF16 = jnp.bfloat16

NORM_EPS = 1e-6
LN_EPS = 1e-5

XA_HEADS = 4
CONV_WIDTH = 31
CONV_HALO = 32
RWKV_HEAD_DIM = 64
RWKV_GN_EPS = RWKV_HEAD_DIM * 1e-5
SGU_CHUNK = 128
SGU_GROUPS = 8
RET_HEADS = 4
RET_CHUNK = 256

LANES = 128
SUBLANES = 8
VMEM_LIMIT_BYTES = 56 * 1024 * 1024


def _cparams(*sem):
    return pltpu.CompilerParams(dimension_semantics=sem, vmem_limit_bytes=VMEM_LIMIT_BYTES)


def _rms(x, g):
    return x * lax.rsqrt(jnp.mean(x * x, axis=-1, keepdims=True) + NORM_EPS) * g


def _layernorm(x, g, b, eps):
    mu = jnp.mean(x, axis=-1, keepdims=True)
    xc = x - mu
    var = jnp.mean(xc * xc, axis=-1, keepdims=True)
    return xc * lax.rsqrt(var + eps) * g + b


def _dot(a, b):
    return jnp.dot(a, b, preferred_element_type=F32)


def _dot_nt(a, b):
    return lax.dot_general(a, b, (((1,), (1,)), ((), ())), preferred_element_type=F32)


def _split2(x):
    hi = x.astype(BF16)
    lo = (x - hi.astype(F32)).astype(BF16)
    return hi, lo


def _dot_split(x, m):
    hi, lo = _split2(x)
    return _dot(hi, m) + _dot(lo, m)


def _row2(v):
    return v.reshape(1, -1).astype(F32)


def _norm_mm_kernel(*refs, act, glu, has_bias):
    it = iter(refs)
    x_ref, g_ref, w_ref = next(it), next(it), next(it)
    w2_ref = next(it) if glu else None
    b_ref = next(it) if has_bias else None
    b2_ref = next(it) if (glu and has_bias) else None
    o_ref, xn_ref = next(it), next(it)

    @pl.when(pl.program_id(1) == 0)
    def _():
        xn_ref[...] = _rms(x_ref[...], g_ref[...]).astype(BF16)

    xn = xn_ref[...]
    y = _dot(xn, w_ref[...])
    if has_bias:
        y = y + b_ref[...]
    if glu:
        y2 = _dot(xn, w2_ref[...])
        if has_bias:
            y2 = y2 + b2_ref[...]
        y = y * jax.nn.sigmoid(y2)
    elif act == "gelu":
        y = 0.5 * y * (1.0 + lax.erf(y * math.sqrt(0.5)))
    o_ref[...] = y.astype(o_ref.dtype)


def _norm_mm(x2d, g, w, bias=None, *, act=None, glu=False, out_dtype=BF16, tm=1024, tn=512, name):
    m, k = x2d.shape
    n_w = w.shape[1]
    n_out = n_w // 2 if glu else n_w
    tm, tn = min(tm, m), min(tn, n_out)
    assert m % tm == 0 and n_out % tn == 0
    nb = n_out // tn
    has_bias = bias is not None
    args = [x2d, _row2(g), w]
    specs = [pl.BlockSpec((tm, k), lambda i, j: (i, 0)),
             pl.BlockSpec((1, k), lambda i, j: (0, 0)),
             pl.BlockSpec((k, tn), lambda i, j: (0, j))]
    if glu:
        args.append(w)
        specs.append(pl.BlockSpec((k, tn), lambda i, j: (0, j + nb)))
    if has_bias:
        b2d = _row2(bias)
        args.append(b2d)
        specs.append(pl.BlockSpec((1, tn), lambda i, j: (0, j)))
        if glu:
            args.append(b2d)
            specs.append(pl.BlockSpec((1, tn), lambda i, j: (0, j + nb)))
    return pl.pallas_call(
        functools.partial(_norm_mm_kernel, act=act, glu=glu, has_bias=has_bias),
        out_shape=jax.ShapeDtypeStruct((m, n_out), out_dtype),
        grid=(m // tm, nb),
        in_specs=specs,
        out_specs=pl.BlockSpec((tm, tn), lambda i, j: (i, j)),
        scratch_shapes=[pltpu.VMEM((tm, k), BF16)],
        compiler_params=_cparams("parallel", "arbitrary"),
        name=name,
    )(*args)


def _mm_res_kernel(a_ref, w_ref, r_ref, o_ref):
    o_ref[...] = r_ref[...] + _dot(a_ref[...], w_ref[...])


def _mm_res(a, w, res, *, tm=1024, name):
    m, k = a.shape
    n = w.shape[1]
    assert m % tm == 0
    return pl.pallas_call(
        _mm_res_kernel,
        out_shape=jax.ShapeDtypeStruct((m, n), F32),
        grid=(m // tm,),
        in_specs=[pl.BlockSpec((tm, k), lambda i: (i, 0)),
                  pl.BlockSpec((k, n), lambda i: (0, 0)),
                  pl.BlockSpec((tm, n), lambda i: (i, 0))],
        out_specs=pl.BlockSpec((tm, n), lambda i: (i, 0)),
        compiler_params=_cparams("parallel"),
        name=name,
    )(a, w, res)


def _xattn_kernel(x_ref, g_ref, wq_ref, kv_ref, wo_ref, o_ref, oh_ref, *, heads):
    x = x_ref[0]
    d = x.shape[-1]
    hd = d // heads
    xn = _rms(x, g_ref[...]).astype(BF16)
    q = _dot(xn, wq_ref[...])
    for h in range(heads):
        qh = q[:, h * hd:(h + 1) * hd].astype(BF16)
        kh = kv_ref[0, :, h * hd:(h + 1) * hd]
        vh = kv_ref[0, :, d + h * hd:d + (h + 1) * hd]
        s = _dot_nt(qh, kh) * (hd ** -0.5)
        p = jnp.exp(s - jnp.max(s, axis=-1, keepdims=True))
        p = p / jnp.sum(p, axis=-1, keepdims=True)
        oh_ref[:, h * hd:(h + 1) * hd] = _dot(p.astype(BF16), vh).astype(BF16)
    o_ref[0] = x + _dot(oh_ref[...], wo_ref[...])


def _xattn(x, g, wq, kv, wo, *, tm=512, name):
    b, t, d = x.shape
    m = kv.shape[1]
    return pl.pallas_call(
        functools.partial(_xattn_kernel, heads=XA_HEADS),
        out_shape=jax.ShapeDtypeStruct((b, t, d), F32),
        grid=(b, t // tm),
        in_specs=[pl.BlockSpec((1, tm, d), lambda bi, i: (bi, i, 0)),
                  pl.BlockSpec((1, d), lambda bi, i: (0, 0)),
                  pl.BlockSpec((d, d), lambda bi, i: (0, 0)),
                  pl.BlockSpec((1, m, 2 * d), lambda bi, i: (bi, 0, 0)),
                  pl.BlockSpec((d, d), lambda bi, i: (0, 0))],
        out_specs=pl.BlockSpec((1, tm, d), lambda bi, i: (bi, i, 0)),
        scratch_shapes=[pltpu.VMEM((tm, d), BF16)],
        compiler_params=_cparams("parallel", "parallel"),
        name=name,
    )(x, _row2(g), wq, kv, wo)


def _swiglu_kernel(*refs, final):
    if final:
        x_ref, g_ref, wg_ref, wu_ref, wd_ref, fg_ref, o_ref, xn_ref, acc_ref = refs
    else:
        x_ref, g_ref, wg_ref, wu_ref, wd_ref, o_ref, xn_ref, acc_ref = refs
    f = pl.program_id(1)

    @pl.when(f == 0)
    def _():
        xn_ref[...] = _rms(x_ref[...], g_ref[...]).astype(BF16)
        acc_ref[...] = jnp.zeros_like(acc_ref)

    xn = xn_ref[...]
    gt = _dot(xn, wg_ref[...])
    up = _dot(xn, wu_ref[...])
    h = (gt * jax.nn.sigmoid(gt) * up).astype(BF16)
    acc_ref[...] += _dot(h, wd_ref[...])

    @pl.when(f == pl.num_programs(1) - 1)
    def _():
        y = x_ref[...] + acc_ref[...]
        if final:
            y = _rms(y, fg_ref[...])
        o_ref[...] = y


def _swiglu(x2d, g, w_gate_up, w_down, final_g=None, *, tm=512, name):
    m, d = x2d.shape
    ff = w_down.shape[0]
    nf = 2
    tf = ff // nf
    assert tf % LANES == 0 and m % tm == 0
    final = final_g is not None
    args = [x2d, _row2(g), w_gate_up, w_gate_up, w_down]
    specs = [pl.BlockSpec((tm, d), lambda i, f: (i, 0)),
             pl.BlockSpec((1, d), lambda i, f: (0, 0)),
             pl.BlockSpec((d, tf), lambda i, f: (0, f)),
             pl.BlockSpec((d, tf), lambda i, f: (0, f + nf)),
             pl.BlockSpec((tf, d), lambda i, f: (f, 0))]
    if final:
        args.append(_row2(final_g))
        specs.append(pl.BlockSpec((1, d), lambda i, f: (0, 0)))
    return pl.pallas_call(
        functools.partial(_swiglu_kernel, final=final),
        out_shape=jax.ShapeDtypeStruct((m, d), F32),
        grid=(m // tm, nf),
        in_specs=specs,
        out_specs=pl.BlockSpec((tm, d), lambda i, f: (i, 0)),
        scratch_shapes=[pltpu.VMEM((tm, d), BF16), pltpu.VMEM((tm, d), F32)],
        compiler_params=_cparams("parallel", "arbitrary"),
        name=name,
    )(*args)


def _conv_tail_kernel(z_ref, zh_ref, dw_ref, dwb_ref, lg_ref, lb_ref, w_ref, b_ref, x_ref,
                      o_ref, zz_ref, a_ref, *, rows):
    tm = z_ref.shape[1]
    not_first = (pl.program_id(1) != 0).astype(F32)
    zz_ref[0:CONV_HALO, :] = zh_ref[0].astype(F32) * not_first
    zz_ref[CONV_HALO:, :] = z_ref[0].astype(F32)
    base = CONV_HALO - (CONV_WIDTH - 1)

    def chunk(c, carry):
        r0 = pl.multiple_of(c * rows, rows)
        win = zz_ref[pl.ds(r0, rows + CONV_HALO), :]
        acc = jnp.zeros((rows, z_ref.shape[2]), F32) + dwb_ref[...]
        for k in range(CONV_WIDTH):
            acc = acc + win[base + k:base + k + rows, :] * dw_ref[k:k + 1, :]
        y = _layernorm(acc, lg_ref[...], lb_ref[...], LN_EPS)
        a_ref[pl.ds(r0, rows), :] = (y * jax.nn.sigmoid(y)).astype(BF16)
        return carry

    lax.fori_loop(0, tm // rows, chunk, 0)
    o_ref[0] = x_ref[0] + _dot(a_ref[...], w_ref[...]) + b_ref[...]


def _conv_tail(z, dw, dw_b, ln_g, ln_b, w_out, b_out, x, *, tm=512, rows=16, name):
    b, t, d = x.shape
    hb = tm // CONV_HALO
    vec = pl.BlockSpec((1, d), lambda bi, i: (0, 0))
    return pl.pallas_call(
        functools.partial(_conv_tail_kernel, rows=rows),
        out_shape=jax.ShapeDtypeStruct((b, t, d), F32),
        grid=(b, t // tm),
        in_specs=[pl.BlockSpec((1, tm, d), lambda bi, i: (bi, i, 0)),
                  pl.BlockSpec((1, CONV_HALO, d), lambda bi, i: (bi, jnp.maximum(i * hb - 1, 0), 0)),
                  pl.BlockSpec((CONV_WIDTH, d), lambda bi, i: (0, 0)),
                  vec, vec, vec,
                  pl.BlockSpec((d, d), lambda bi, i: (0, 0)),
                  vec,
                  pl.BlockSpec((1, tm, d), lambda bi, i: (bi, i, 0))],
        out_specs=pl.BlockSpec((1, tm, d), lambda bi, i: (bi, i, 0)),
        scratch_shapes=[pltpu.VMEM((tm + CONV_HALO, d), F32), pltpu.VMEM((tm, d), BF16)],
        compiler_params=_cparams("parallel", "parallel"),
        name=name,
    )(z, z, dw.astype(F32), _row2(dw_b), _row2(ln_g), _row2(ln_b), w_out, _row2(b_out), x)


def _softplus(x):
    return jnp.maximum(x, 0.0) + jnp.log(1.0 + jnp.exp(-jnp.abs(x)))


def _rwkv_prep_kernel(x_ref, xh_ref, g_ref, mix_ref, wr_ref, wk_ref, wv_ref, w0_ref, w1_ref, w2_ref,
                      a0_ref, a1_ref, a2_ref, g1_ref, g2_ref, kk_ref, ka_ref, rk_ref, e_ref, e2_ref, et_ref,
                      dec_o, an_o, b_o, k_o, wr_o, v_o, bv_o, g_o, s_o, hs_ref):
    tm = x_ref.shape[1]
    gn = g_ref[...]
    hn = _rms(x_ref[0], gn)
    not_first = (pl.program_id(1) != 0).astype(F32)
    hs_ref[0:SUBLANES, :] = _rms(xh_ref[0], gn) * not_first
    hs_ref[SUBLANES:, :] = hn
    delta = hs_ref[pl.ds(SUBLANES - 1, tm), :] - hn

    def mixed(c):
        return (hn + delta * mix_ref[c:c + 1, :]).astype(BF16)

    r = _dot(mixed(0), wr_ref[...])
    k = _dot(mixed(2), wk_ref[...])
    v = _dot(mixed(3), wv_ref[...])
    w_lora = _dot(jnp.tanh(_dot(mixed(1), w1_ref[...])).astype(BF16), w2_ref[...])
    w = -_softplus(-(w0_ref[...] + w_lora)) - 0.5
    decay = jnp.exp(-jnp.exp(w))
    a = jax.nn.sigmoid(a0_ref[...] + _dot(_dot(mixed(4), a1_ref[...]).astype(BF16), a2_ref[...]))
    gate = _dot(jax.nn.sigmoid(_dot(mixed(5), g1_ref[...])).astype(BF16), g2_ref[...])

    e, e2, et = e_ref[...], e2_ref[...], et_ref[...]
    kk = k * kk_ref[...]
    ss = _dot_split(_dot_split(kk * kk, e), et)
    kk = kk / jnp.maximum(jnp.sqrt(ss), 1e-12)
    k = k * (1.0 + (a - 1.0) * ka_ref[...])
    b = kk * a
    bonus = _dot_split(_dot_split(r * k * rk_ref[...], e), et)

    dec_o[0] = decay
    an_o[0] = -kk
    b_o[0] = b
    k_o[0] = k
    wr_o[0] = decay * r
    v_o[0] = v
    bv_o[0] = bonus * v
    g_o[0] = gate.astype(g_o.dtype)
    s_o[0] = _dot_split(b * r, e) + _dot_split(k * r, e2)


def _rwkv_prep(x, g, p, e, e2, et, *, tm=256, name):
    b, t, d = x.shape
    hb = tm // SUBLANES
    tile = pl.BlockSpec((1, tm, d), lambda bi, i: (bi, i, 0))
    vec = pl.BlockSpec((1, d), lambda bi, i: (0, 0))

    def full(a):
        return pl.BlockSpec(a.shape, lambda bi, i: (0,) * a.ndim)

    consts = [p["mix"], p["wr"], p["wk"], p["wv"], p["w0"], p["w1"], p["w2"], p["a0"], p["a1"], p["a2"],
              p["g1"], p["g2"], p["k_k"], p["k_a"], p["r_k"], e, e2, et]
    f32_tile = jax.ShapeDtypeStruct((b, t, d), F32)
    return pl.pallas_call(
        _rwkv_prep_kernel,
        out_shape=[f32_tile] * 7 + [jax.ShapeDtypeStruct((b, t, d), BF16),
                                    jax.ShapeDtypeStruct((b, t, LANES), F32)],
        grid=(b, t // tm),
        in_specs=[tile,
                  pl.BlockSpec((1, SUBLANES, d), lambda bi, i: (bi, jnp.maximum(i * hb - 1, 0), 0)),
                  vec] + [full(a) for a in consts],
        out_specs=[tile] * 8 + [pl.BlockSpec((1, tm, LANES), lambda bi, i: (bi, i, 0))],
        scratch_shapes=[pltpu.VMEM((tm + SUBLANES, d), F32)],
        compiler_params=_cparams("parallel", "parallel"),
        name=name,
    )(x, x, _row2(g), *consts)


def _rwkv_scan_kernel(dec_ref, an_ref, b_ref, k_ref, wr_ref, v_ref, sc_ref, y_ref, s_ref):
    tb, nk, nh = dec_ref.shape

    @pl.when(pl.program_id(0) == 0)
    def _():
        s_ref[...] = jnp.zeros_like(s_ref)

    def step(t, carry):
        for hb in range(nh // LANES):
            ln = slice(hb * LANES, (hb + 1) * LANES)
            sa = jnp.zeros(s_ref.shape[1:2] + (LANES,), F32)
            ys = jnp.zeros_like(sa)
            for j in range(nk):
                s = s_ref[j, :, ln]
                sa = sa + s * an_ref[t, j:j + 1, ln]
                ys = ys + s * wr_ref[t, j:j + 1, ln]
            v = v_ref[t, :, ln]
            y_ref[t, :, ln] = ys + sa * sc_ref[t, 0:1, ln] + v * sc_ref[t, 1:2, ln]
            for j in range(nk):
                s_ref[j, :, ln] = (s_ref[j, :, ln] * dec_ref[t, j:j + 1, ln]
                                   + sa * b_ref[t, j:j + 1, ln] + v * k_ref[t, j:j + 1, ln])
        return carry

    lax.fori_loop(0, tb, step, 0)


def _rwkv_scan(dec, an, bb, kk, wr, v, sc, *, tb=16, name):
    t, n, nh = dec.shape
    blk = pl.BlockSpec((tb, n, nh), lambda i: (i, 0, 0))
    row = pl.BlockSpec((tb, 2, nh), lambda i: (i, 0, 0))
    return pl.pallas_call(
        _rwkv_scan_kernel,
        out_shape=jax.ShapeDtypeStruct((t, n, nh), F32),
        grid=(t // tb,),
        in_specs=[blk] * 6 + [row],
        out_specs=blk,
        scratch_shapes=[pltpu.VMEM((n, n, nh), F32)],
        compiler_params=_cparams("arbitrary"),
        name=name,
    )(dec, an, bb, kk, wr, v, sc)


def _rwkv_post_kernel(y_ref, bv_ref, g_ref, x_ref, gg_ref, gb_ref, e_ref, et_ref, w_ref, o_ref):
    e, et = e_ref[...], et_ref[...]
    y = y_ref[...]
    inv_n = 1.0 / RWKV_HEAD_DIM
    mu = _dot_split(_dot_split(y, e) * inv_n, et)
    yc = y - mu
    var = _dot_split(_dot_split(yc * yc, e) * inv_n, et)
    yn = yc * lax.rsqrt(var + RWKV_GN_EPS) * gg_ref[...] + gb_ref[...]
    z = ((yn + bv_ref[...]) * g_ref[...].astype(F32)).astype(BF16)
    o_ref[...] = x_ref[...] + _dot(z, w_ref[...])


def _rwkv_post(y, bv, g, x2d, gn_g, gn_b, e, et, w_o, *, tm=512, name):
    m, d = x2d.shape
    tile = pl.BlockSpec((tm, d), lambda i: (i, 0))
    vec = pl.BlockSpec((1, d), lambda i: (0, 0))
    return pl.pallas_call(
        _rwkv_post_kernel,
        out_shape=jax.ShapeDtypeStruct((m, d), F32),
        grid=(m // tm,),
        in_specs=[tile, tile, tile, tile, vec, vec,
                  pl.BlockSpec(e.shape, lambda i: (0, 0)),
                  pl.BlockSpec(et.shape, lambda i: (0, 0)),
                  pl.BlockSpec((d, d), lambda i: (0, 0))],
        out_specs=tile,
        compiler_params=_cparams("parallel"),
        name=name,
    )(y, bv, g, x2d, _row2(gn_g), _row2(gn_b), e, et, w_o)


def _rwkv_mixer(x, g, p):
    b, t, d = x.shape
    n = RWKV_HEAD_DIM
    h = d // n
    lane = jnp.arange(LANES)[None, :]
    head = (jnp.arange(d) // n)[:, None]
    e = (head == lane).astype(BF16)
    e2 = (head + h == lane).astype(BF16)
    et = e.T
    dec, an, bb, kk, wr, v, bv, gate, s = _rwkv_prep(x, g, p, e, e2, et, name="rwkv_prep")

    def tm(a):
        return a.reshape(b, t, h, n).transpose(1, 3, 0, 2).reshape(t, n, b * h)

    sc = s[:, :, :2 * h].reshape(b, t, 2, h).transpose(1, 2, 0, 3).reshape(t, 2, b * h)
    y = _rwkv_scan(tm(dec), tm(an), tm(bb), tm(kk), tm(wr), tm(v), sc, name="rwkv_scan")
    y = y.reshape(t, n, b, h).transpose(2, 0, 3, 1).reshape(b * t, d)
    out = _rwkv_post(y, bv.reshape(b * t, d), gate.reshape(b * t, d), x.reshape(b * t, d),
                     p["gn_g"], p["gn_b"], e, et, p["w_o"], name="rwkv_post")
    return out.reshape(b, t, d)


def _sgu_tail_kernel(u_ref, v_ref, lg_ref, lb_ref, ws_ref, bst_ref, w_ref, x_ref, o_ref, vn_ref, gt_ref):
    tm, width = u_ref.shape[1], u_ref.shape[2]
    c = SGU_CHUNK
    gd = width // SGU_GROUPS
    vn_ref[...] = _layernorm(v_ref[0].astype(F32), lg_ref[...], lb_ref[...], LN_EPS).astype(BF16)
    tri = lax.broadcasted_iota(jnp.int32, (c, c), 0) >= lax.broadcasted_iota(jnp.int32, (c, c), 1)
    for gi in range(SGU_GROUPS):
        wsm = jnp.where(tri, ws_ref[gi], 0.0).astype(BF16)
        bias = bst_ref[:, gi:gi + 1]
        cols = slice(gi * gd, (gi + 1) * gd)
        for ci in range(tm // c):
            rows = slice(ci * c, (ci + 1) * c)
            mixed = _dot(wsm, vn_ref[rows, cols]) + bias
            gt_ref[rows, cols] = (u_ref[0, rows, cols].astype(F32) * mixed).astype(BF16)
    o_ref[0] = x_ref[0] + _dot(gt_ref[...], w_ref[...])


def _sgu_tail(uv, ln_g, ln_b, ws, bs, w_out, x, *, tm=512, name):
    b, t, d = x.shape
    width = uv.shape[-1] // 2
    return pl.pallas_call(
        _sgu_tail_kernel,
        out_shape=jax.ShapeDtypeStruct((b, t, d), F32),
        grid=(b, t // tm),
        in_specs=[pl.BlockSpec((1, tm, width), lambda bi, i: (bi, i, 0)),
                  pl.BlockSpec((1, tm, width), lambda bi, i: (bi, i, 1)),
                  pl.BlockSpec((1, width), lambda bi, i: (0, 0)),
                  pl.BlockSpec((1, width), lambda bi, i: (0, 0)),
                  pl.BlockSpec(ws.shape, lambda bi, i: (0, 0, 0)),
                  pl.BlockSpec((SGU_CHUNK, SGU_GROUPS), lambda bi, i: (0, 0)),
                  pl.BlockSpec((width, d), lambda bi, i: (0, 0)),
                  pl.BlockSpec((1, tm, d), lambda bi, i: (bi, i, 0))],
        out_specs=pl.BlockSpec((1, tm, d), lambda bi, i: (bi, i, 0)),
        scratch_shapes=[pltpu.VMEM((tm, width), BF16), pltpu.VMEM((tm, width), BF16)],
        compiler_params=_cparams("parallel", "parallel"),
        name=name,
    )(uv, uv, _row2(ln_g), _row2(ln_b), ws.astype(F32), bs.T.astype(F32), w_out, x)


def _rotate_every_two(x):
    n = x.shape[-1]
    even = lax.broadcasted_iota(jnp.int32, x.shape, 1) % 2 == 0
    return jnp.where(even, -pltpu.roll(x, n - 1, 1), pltpu.roll(x, 1, 1))


def _ret_core_kernel(q_ref, k_ref, v_ref, g_ref, sin_ref, cos_ref, dm_ref, qd_ref, kd_ref, cd_ref,
                     o_ref, r_ref):
    @pl.when(pl.program_id(2) == 0)
    def _():
        r_ref[...] = jnp.zeros_like(r_ref)

    sin, cos = sin_ref[...], cos_ref[...]
    q = q_ref[0].astype(F32)
    k = k_ref[0].astype(F32) * (q.shape[-1] ** -0.5)
    q = q * cos + _rotate_every_two(q) * sin
    k = k * cos + _rotate_every_two(k) * sin
    vb = v_ref[0]
    scores = _dot_nt(q.astype(BF16), k.astype(BF16)) * dm_ref[0]
    inner = _dot(scores.astype(BF16), vb)
    r = r_ref[...]
    cross = _dot((q * qd_ref[0]).astype(BF16), r.astype(BF16))
    ks_t = jnp.transpose(k * kd_ref[0]).astype(BF16)
    r_ref[...] = r * cd_ref[0, 0:1, 0:1] + _dot(ks_t, vb)
    o = inner + cross
    o = o * lax.rsqrt(jnp.mean(o * o, axis=-1, keepdims=True) + NORM_EPS)
    gate = g_ref[0].astype(F32)
    o_ref[0] = (gate * jax.nn.sigmoid(gate) * o).astype(o_ref.dtype)


def _ret_core(qkvg, t, d, *, name):
    b = qkvg.shape[0]
    hh, c = RET_HEADS, RET_CHUNK
    dk = d // hh
    dv = 2 * dk
    angle = jnp.repeat(1.0 / (10000.0 ** jnp.linspace(0.0, 1.0, dk // 2, dtype=F32)), 2)
    theta = jnp.arange(t, dtype=F32)[:, None] * angle[None]
    log_gamma = jnp.log(1.0 - 2.0 ** (-5.0 - jnp.arange(hh, dtype=F32)))
    idx = jnp.arange(c, dtype=F32)
    diff = idx[:, None] - idx[None, :]
    dm = jnp.where(diff[None] >= 0, jnp.exp(log_gamma[:, None, None] * jnp.maximum(diff, 0.0)[None]), 0.0)
    qd = jnp.exp(log_gamma[:, None] * (idx + 1.0))[:, :, None]
    kd = jnp.exp(log_gamma[:, None] * (c - 1.0 - idx))[:, :, None]
    cd = jnp.broadcast_to(jnp.exp(log_gamma * c)[:, None, None], (hh, SUBLANES, LANES))
    nq = d // dk
    nv = 2 * d // dv
    return pl.pallas_call(
        _ret_core_kernel,
        out_shape=jax.ShapeDtypeStruct((b, t, hh * dv), BF16),
        grid=(b, hh, t // c),
        in_specs=[pl.BlockSpec((1, c, dk), lambda bi, h, n: (bi, n, h)),
                  pl.BlockSpec((1, c, dk), lambda bi, h, n: (bi, n, nq + h)),
                  pl.BlockSpec((1, c, dv), lambda bi, h, n: (bi, n, nv + h)),
                  pl.BlockSpec((1, c, dv), lambda bi, h, n: (bi, n, nv + hh + h)),
                  pl.BlockSpec((c, dk), lambda bi, h, n: (n, 0)),
                  pl.BlockSpec((c, dk), lambda bi, h, n: (n, 0)),
                  pl.BlockSpec((1, c, c), lambda bi, h, n: (h, 0, 0)),
                  pl.BlockSpec((1, c, 1), lambda bi, h, n: (h, 0, 0)),
                  pl.BlockSpec((1, c, 1), lambda bi, h, n: (h, 0, 0)),
                  pl.BlockSpec((1, SUBLANES, LANES), lambda bi, h, n: (h, 0, 0))],
        out_specs=pl.BlockSpec((1, c, dv), lambda bi, h, n: (bi, n, h)),
        scratch_shapes=[pltpu.VMEM((dk, dv), F32)],
        compiler_params=_cparams("parallel", "parallel", "arbitrary"),
        name=name,
    )(qkvg, qkvg, qkvg, qkvg, jnp.sin(theta), jnp.cos(theta), dm, qd, kd, cd)


def _pad_axis(a, axis, to):
    pad = [(0, 0)] * a.ndim
    pad[axis] = (0, to - a.shape[axis])
    return jnp.pad(a, pad)


def kernel(x, mem, mix_norm_g, xattn_norm_g, mem_norm_g, xattn_wq, xattn_wkv, xattn_wo, ffn_norm_g, ffn_w_gate_up, ffn_w_down, conv_w_in, conv_b_in, conv_dw, conv_dw_b, conv_ln_g, conv_ln_b, conv_w_out, conv_b_out, rwkv_mix, rwkv_w_rkv, rwkv_w0, rwkv_w1, rwkv_w2, rwkv_a0, rwkv_a1, rwkv_a2, rwkv_g1, rwkv_g2, rwkv_k_k, rwkv_k_a, rwkv_r_k, rwkv_gn_g, rwkv_gn_b, rwkv_w_o, sgu_w_in, sgu_b_in, sgu_ln_g, sgu_ln_b, sgu_ws, sgu_bs, sgu_w_out, ret_w_in, ret_w_out, final_norm_g):
    b, t, d = x.shape
    n_mem = mem.shape[1]
    depth = mix_norm_g.shape[0]
    m = b * t
    mem2d = mem.reshape(b * n_mem, d)

    for i in range(depth):
        mixer, j = i % 4, i // 4
        g = mix_norm_g[i]
        if mixer == 0:
            z = _norm_mm(x.reshape(m, d), g, conv_w_in[j].astype(BF16), conv_b_in[j], glu=True,
                         name="conv_in")
            x = _conv_tail(z.reshape(b, t, d), conv_dw[j], conv_dw_b[j], conv_ln_g[j], conv_ln_b[j],
                           conv_w_out[j].astype(BF16), conv_b_out[j], x, name="conv_tail")
        elif mixer == 1:
            lw = _pad_axis(rwkv_w1[j], 1, LANES)
            la = _pad_axis(rwkv_a1[j], 1, LANES)
            lg = _pad_axis(rwkv_g1[j], 1, 2 * LANES)
            p = dict(mix=rwkv_mix[j].astype(F32),
                     wr=rwkv_w_rkv[j, 0].astype(BF16), wk=rwkv_w_rkv[j, 1].astype(BF16),
                     wv=rwkv_w_rkv[j, 2].astype(BF16),
                     w0=_row2(rwkv_w0[j]), w1=lw.astype(BF16),
                     w2=_pad_axis(rwkv_w2[j], 0, LANES).astype(BF16),
                     a0=_row2(rwkv_a0[j]), a1=la.astype(BF16),
                     a2=_pad_axis(rwkv_a2[j], 0, LANES).astype(BF16),
                     g1=lg.astype(BF16), g2=_pad_axis(rwkv_g2[j], 0, 2 * LANES).astype(BF16),
                     k_k=_row2(rwkv_k_k[j]), k_a=_row2(rwkv_k_a[j]), r_k=_row2(rwkv_r_k[j]),
                     gn_g=rwkv_gn_g[j], gn_b=rwkv_gn_b[j], w_o=rwkv_w_o[j].astype(BF16))
            x = _rwkv_mixer(x, g, p)
        elif mixer == 2:
            uv = _norm_mm(x.reshape(m, d), g, sgu_w_in[j].astype(BF16), sgu_b_in[j], act="gelu",
                          name="sgu_in")
            x = _sgu_tail(uv.reshape(b, t, -1), sgu_ln_g[j], sgu_ln_b[j], sgu_ws[j], sgu_bs[j],
                          sgu_w_out[j].astype(BF16), x, name="sgu_tail")
        else:
            qkvg = _norm_mm(x.reshape(m, d), g, ret_w_in[j].astype(BF16), name="ret_in")
            o = _ret_core(qkvg.reshape(b, t, -1), t, d, name="ret_core")
            x = _mm_res(o.reshape(m, -1), ret_w_out[j].astype(BF16), x.reshape(m, d),
                        name="ret_out").reshape(b, t, d)

        kv = _norm_mm(mem2d, mem_norm_g[i], xattn_wkv[i].astype(BF16), name="xattn_kv")
        x = _xattn(x, xattn_norm_g[i], xattn_wq[i].astype(BF16), kv.reshape(b, n_mem, 2 * d),
                   xattn_wo[i].astype(BF16), name="xattn")
        fg = final_norm_g if i == depth - 1 else None
        x = _swiglu(x.reshape(m, d), ffn_norm_g[i], ffn_w_gate_up[i].astype(BF16),
                    ffn_w_down[i].astype(BF16), fg, name="swiglu").reshape(b, t, d)
    return x
```

```python
import functools
import math

import jax
import jax.numpy as jnp
from jax import lax
from jax.experimental import pallas as pl
from jax.experimental.pallas import tpu as pltpu

F32 = jnp.float32
BF16 = jnp.bfloat16

NORM_EPS = 1e-6
LN_EPS = 1e-5

XA_HEADS = 4
CONV_WIDTH = 31
CONV_HALO = 32
RWKV_HEAD_DIM = 64
RWKV_GN_EPS = RWKV_HEAD_DIM * 1e-5
SGU_CHUNK = 128
SGU_GROUPS = 8
RET_HEADS = 4
RET_CHUNK = 256

LANES = 128
SUBLANES = 8
VMEM_LIMIT_BYTES = 56 * 1024 * 1024


def _cparams(*sem):
    return pltpu.CompilerParams(dimension_semantics=sem, vmem_limit_bytes=VMEM_LIMIT_BYTES)


def _rms(x, g):
    return x * lax.rsqrt(jnp.mean(x * x, axis=-1, keepdims=True) + NORM_EPS) * g


def _layernorm(x, g, b, eps):
    mu = jnp.mean(x, axis=-1, keepdims=True)
    xc = x - mu
    var = jnp.mean(xc * xc, axis=-1, keepdims=True)
    return xc * lax.rsqrt(var + eps) * g + b


def _dot(a, b):
    return jnp.dot(a, b, preferred_element_type=F32)


def _dot_nt(a, b):
    return lax.dot_general(a, b, (((1,), (1,)), ((), ())), preferred_element_type=F32)


def _head_sum(x, e):
    return _dot(x.astype(BF16), e)


def _row2(v):
    return v.reshape(1, -1).astype(F32)


def _norm_mm_kernel(*refs, act, glu, has_bias):
    it = iter(refs)
    x_ref, g_ref, w_ref = next(it), next(it), next(it)
    w2_ref = next(it) if glu else None
    b_ref = next(it) if has_bias else None
    b2_ref = next(it) if (glu and has_bias) else None
    o_ref, xn_ref = next(it), next(it)

    @pl.when(pl.program_id(1) == 0)
    def _():
        xn_ref[...] = _rms(x_ref[...], g_ref[...]).astype(BF16)

    xn = xn_ref[...]
    y = _dot(xn, w_ref[...])
    if has_bias:
        y = y + b_ref[...]
    if glu:
        y2 = _dot(xn, w2_ref[...])
        if has_bias:
            y2 = y2 + b2_ref[...]
        y = y * jax.nn.sigmoid(y2)
    elif act == "gelu":
        y = 0.5 * y * (1.0 + lax.erf(y * math.sqrt(0.5)))
    o_ref[...] = y.astype(o_ref.dtype)


def _norm_mm(x2d, g, w, bias=None, *, act=None, glu=False, out_dtype=BF16, tm=1024, tn=1024, name):
    m, k = x2d.shape
    n_w = w.shape[1]
    n_out = n_w // 2 if glu else n_w
    tm, tn = min(tm, m), min(tn, n_out)
    assert m % tm == 0 and n_out % tn == 0
    nb = n_out // tn
    has_bias = bias is not None
    args = [x2d, _row2(g), w]
    specs = [pl.BlockSpec((tm, k), lambda i, j: (i, 0)),
             pl.BlockSpec((1, k), lambda i, j: (0, 0)),
             pl.BlockSpec((k, tn), lambda i, j: (0, j))]
    if glu:
        args.append(w)
        specs.append(pl.BlockSpec((k, tn), lambda i, j: (0, j + nb)))
    if has_bias:
        b2d = _row2(bias)
        args.append(b2d)
        specs.append(pl.BlockSpec((1, tn), lambda i, j: (0, j)))
        if glu:
            args.append(b2d)
            specs.append(pl.BlockSpec((1, tn), lambda i, j: (0, j + nb)))
    return pl.pallas_call(
        functools.partial(_norm_mm_kernel, act=act, glu=glu, has_bias=has_bias),
        out_shape=jax.ShapeDtypeStruct((m, n_out), out_dtype),
        grid=(m // tm, nb),
        in_specs=specs,
        out_specs=pl.BlockSpec((tm, tn), lambda i, j: (i, j)),
        scratch_shapes=[pltpu.VMEM((tm, k), BF16)],
        compiler_params=_cparams("parallel", "arbitrary"),
        name=name,
    )(*args)


def _mm_res_kernel(a_ref, w_ref, r_ref, o_ref):
    o_ref[...] = r_ref[...] + _dot(a_ref[...], w_ref[...])


def _mm_res(a, w, res, *, tm=1024, name):
    m, k = a.shape
    n = w.shape[1]
    assert m % tm == 0
    return pl.pallas_call(
        _mm_res_kernel,
        out_shape=jax.ShapeDtypeStruct((m, n), F32),
        grid=(m // tm,),
        in_specs=[pl.BlockSpec((tm, k), lambda i: (i, 0)),
                  pl.BlockSpec((k, n), lambda i: (0, 0)),
                  pl.BlockSpec((tm, n), lambda i: (i, 0))],
        out_specs=pl.BlockSpec((tm, n), lambda i: (i, 0)),
        compiler_params=_cparams("parallel"),
        name=name,
    )(a, w, res)


def _xattn_kernel(x_ref, g_ref, wq_ref, kv_ref, wo_ref, o_ref, oh_ref, *, heads):
    x = x_ref[0]
    d = x.shape[-1]
    hd = d // heads
    xn = _rms(x, g_ref[...]).astype(BF16)
    q = _dot(xn, wq_ref[...])
    for h in range(heads):
        qh = q[:, h * hd:(h + 1) * hd].astype(BF16)
        kh = kv_ref[0, :, h * hd:(h + 1) * hd]
        vh = kv_ref[0, :, d + h * hd:d + (h + 1) * hd]
        s = _dot_nt(qh, kh) * (hd ** -0.5)
        p = jnp.exp(s - jnp.max(s, axis=-1, keepdims=True))
        p = p / jnp.sum(p, axis=-1, keepdims=True)
        oh_ref[:, h * hd:(h + 1) * hd] = _dot(p.astype(BF16), vh).astype(BF16)
    o_ref[0] = x + _dot(oh_ref[...], wo_ref[...])


def _xattn(x, g, wq, kv, wo, *, tm=512, name):
    b, t, d = x.shape
    m = kv.shape[1]
    return pl.pallas_call(
        functools.partial(_xattn_kernel, heads=XA_HEADS),
        out_shape=jax.ShapeDtypeStruct((b, t, d), F32),
        grid=(b, t // tm),
        in_specs=[pl.BlockSpec((1, tm, d), lambda bi, i: (bi, i, 0)),
                  pl.BlockSpec((1, d), lambda bi, i: (0, 0)),
                  pl.BlockSpec((d, d), lambda bi, i: (0, 0)),
                  pl.BlockSpec((1, m, 2 * d), lambda bi, i: (bi, 0, 0)),
                  pl.BlockSpec((d, d), lambda bi, i: (0, 0))],
        out_specs=pl.BlockSpec((1, tm, d), lambda bi, i: (bi, i, 0)),
        scratch_shapes=[pltpu.VMEM((tm, d), BF16)],
        compiler_params=_cparams("parallel", "parallel"),
        name=name,
    )(x, _row2(g), wq, kv, wo)


def _swiglu_kernel(*refs, final):
    if final:
        x_ref, g_ref, wg_ref, wu_ref, wd_ref, fg_ref, o_ref, xn_ref, acc_ref = refs
    else:
        x_ref, g_ref, wg_ref, wu_ref, wd_ref, o_ref, xn_ref, acc_ref = refs
    f = pl.program_id(1)

    @pl.when(f == 0)
    def _():
        xn_ref[...] = _rms(x_ref[...], g_ref[...]).astype(BF16)
        acc_ref[...] = jnp.zeros_like(acc_ref)

    xn = xn_ref[...]
    gt = _dot(xn, wg_ref[...])
    up = _dot(xn, wu_ref[...])
    h = (gt * jax.nn.sigmoid(gt) * up).astype(BF16)
    acc_ref[...] += _dot(h, wd_ref[...])

    @pl.when(f == pl.num_programs(1) - 1)
    def _():
        y = x_ref[...] + acc_ref[...]
        if final:
            y = _rms(y, fg_ref[...])
        o_ref[...] = y


def _swiglu(x2d, g, w_gate_up, w_down, final_g=None, *, tm=512, name):
    m, d = x2d.shape
    ff = w_down.shape[0]
    nf = 2
    tf = ff // nf
    assert tf % LANES == 0 and m % tm == 0
    final = final_g is not None
    args = [x2d, _row2(g), w_gate_up, w_gate_up, w_down]
    specs = [pl.BlockSpec((tm, d), lambda i, f: (i, 0)),
             pl.BlockSpec((1, d), lambda i, f: (0, 0)),
             pl.BlockSpec((d, tf), lambda i, f: (0, f)),
             pl.BlockSpec((d, tf), lambda i, f: (0, f + nf)),
             pl.BlockSpec((tf, d), lambda i, f: (f, 0))]
    if final:
        args.append(_row2(final_g))
        specs.append(pl.BlockSpec((1, d), lambda i, f: (0, 0)))
    return pl.pallas_call(
        functools.partial(_swiglu_kernel, final=final),
        out_shape=jax.ShapeDtypeStruct((m, d), F32),
        grid=(m // tm, nf),
        in_specs=specs,
        out_specs=pl.BlockSpec((tm, d), lambda i, f: (i, 0)),
        scratch_shapes=[pltpu.VMEM((tm, d), BF16), pltpu.VMEM((tm, d), F32)],
        compiler_params=_cparams("parallel", "arbitrary"),
        name=name,
    )(*args)


def _conv_tail_kernel(z_ref, zh_ref, dw_ref, dwb_ref, lg_ref, lb_ref, w_ref, b_ref, x_ref,
                      o_ref, zz_ref, sh_ref, c_ref, dwx_ref, *, rows):
    tm, d = z_ref.shape[1], z_ref.shape[2]
    not_first = (pl.program_id(1) != 0).astype(F32)
    zz_ref[0:CONV_HALO, :] = zh_ref[0].astype(F32) * not_first
    zz_ref[CONV_HALO:, :] = z_ref[0].astype(F32)
    for s in range(1, SUBLANES):
        sh_ref[s - 1] = zz_ref[pl.ds(s, sh_ref.shape[1]), :]
    for k in range(CONV_WIDTH):
        dwx_ref[k] = jnp.broadcast_to(dw_ref[k:k + 1, :], (SUBLANES, d))
    dwx_ref[CONV_WIDTH] = jnp.broadcast_to(dwb_ref[...], (SUBLANES, d))
    base = CONV_HALO - (CONV_WIDTH - 1)
    groups = rows // SUBLANES

    def chunk(c, carry):
        r0 = pl.multiple_of(c * rows, rows)
        acc = [dwx_ref[CONV_WIDTH]] * groups
        for k in range(CONV_WIDTH):
            blk, s = divmod(base + k, SUBLANES)
            src = zz_ref if s == 0 else sh_ref.at[s - 1]
            wk = dwx_ref[k]
            for gi in range(groups):
                acc[gi] = acc[gi] + src[pl.ds(r0 + (blk + gi) * SUBLANES, SUBLANES), :] * wk
        for gi in range(groups):
            c_ref[pl.ds(r0 + gi * SUBLANES, SUBLANES), :] = acc[gi]
        return carry

    lax.fori_loop(0, tm // rows, chunk, 0)
    y = _layernorm(c_ref[...], lg_ref[...], lb_ref[...], LN_EPS)
    act = (y * jax.nn.sigmoid(y)).astype(BF16)
    o_ref[0] = x_ref[0] + _dot(act, w_ref[...]) + b_ref[...]


def _conv_tail(z, dw, dw_b, ln_g, ln_b, w_out, b_out, x, *, tm=512, rows=32, name):
    b, t, d = x.shape
    hb = tm // CONV_HALO
    vec = pl.BlockSpec((1, d), lambda bi, i: (0, 0))
    return pl.pallas_call(
        functools.partial(_conv_tail_kernel, rows=rows),
        out_shape=jax.ShapeDtypeStruct((b, t, d), F32),
        grid=(b, t // tm),
        in_specs=[pl.BlockSpec((1, tm, d), lambda bi, i: (bi, i, 0)),
                  pl.BlockSpec((1, CONV_HALO, d), lambda bi, i: (bi, jnp.maximum(i * hb - 1, 0), 0)),
                  pl.BlockSpec((CONV_WIDTH, d), lambda bi, i: (0, 0)),
                  vec, vec, vec,
                  pl.BlockSpec((d, d), lambda bi, i: (0, 0)),
                  vec,
                  pl.BlockSpec((1, tm, d), lambda bi, i: (bi, i, 0))],
        out_specs=pl.BlockSpec((1, tm, d), lambda bi, i: (bi, i, 0)),
        scratch_shapes=[pltpu.VMEM((tm + CONV_HALO, d), F32),
                        pltpu.VMEM((SUBLANES - 1, tm + CONV_HALO - SUBLANES, d), F32),
                        pltpu.VMEM((tm, d), F32),
                        pltpu.VMEM((CONV_WIDTH + 1, SUBLANES, d), F32)],
        compiler_params=_cparams("parallel", "parallel"),
        name=name,
    )(z, z, dw.astype(F32), _row2(dw_b), _row2(ln_g), _row2(ln_b), w_out, _row2(b_out), x)


def _softplus(x):
    return jnp.maximum(x, 0.0) + jnp.log(1.0 + jnp.exp(-jnp.abs(x)))


def _rwkv_prep_kernel(x_ref, xh_ref, g_ref, mix_ref, wr_ref, wk_ref, wv_ref, w0_ref, w1_ref, w2_ref,
                      a0_ref, a1_ref, a2_ref, g1_ref, g2_ref, kk_ref, ka_ref, rk_ref, e_ref, e2_ref, et_ref,
                      dec_o, an_o, b_o, k_o, wr_o, v_o, bv_o, g_o, s_o, hs_ref):
    tm = x_ref.shape[1]
    gn = g_ref[...]
    hn = _rms(x_ref[0], gn)
    not_first = (pl.program_id(1) != 0).astype(F32)
    hs_ref[0:SUBLANES, :] = _rms(xh_ref[0], gn) * not_first
    hs_ref[SUBLANES:, :] = hn
    delta = hs_ref[pl.ds(SUBLANES - 1, tm), :] - hn

    def mixed(c):
        return (hn + delta * mix_ref[c:c + 1, :]).astype(BF16)

    r = _dot(mixed(0), wr_ref[...])
    k = _dot(mixed(2), wk_ref[...])
    v = _dot(mixed(3), wv_ref[...])
    w_lora = _dot(jnp.tanh(_dot(mixed(1), w1_ref[...])).astype(BF16), w2_ref[...])
    w = -_softplus(-(w0_ref[...] + w_lora)) - 0.5
    decay = jnp.exp(-jnp.exp(w))
    a = jax.nn.sigmoid(a0_ref[...] + _dot(_dot(mixed(4), a1_ref[...]).astype(BF16), a2_ref[...]))
    gate = _dot(jax.nn.sigmoid(_dot(mixed(5), g1_ref[...])).astype(BF16), g2_ref[...])

    e, e2, et = e_ref[...], e2_ref[...], et_ref[...]
    kk = k * kk_ref[...]
    ss = _dot(_head_sum(kk * kk, e).astype(BF16), et)
    kk = kk / jnp.maximum(jnp.sqrt(ss), 1e-12)
    k = k * (1.0 + (a - 1.0) * ka_ref[...])
    b = kk * a
    bonus = _dot(_head_sum(r * k * rk_ref[...], e).astype(BF16), et)

    dec_o[0] = decay
    an_o[0] = -kk
    b_o[0] = b
    k_o[0] = k
    wr_o[0] = decay * r
    v_o[0] = v
    bv_o[0] = bonus * v
    g_o[0] = gate.astype(g_o.dtype)
    s_o[0] = _head_sum(b * r, e) + _head_sum(k * r, e2)


def _rwkv_prep(x, g, p, e, e2, et, *, tm=256, name):
    b, t, d = x.shape
    hb = tm // SUBLANES
    tile = pl.BlockSpec((1, tm, d), lambda bi, i: (bi, i, 0))
    vec = pl.BlockSpec((1, d), lambda bi, i: (0, 0))

    def full(a):
        return pl.BlockSpec(a.shape, lambda bi, i: (0,) * a.ndim)

    consts = [p["mix"], p["wr"], p["wk"], p["wv"], p["w0"], p["w1"], p["w2"], p["a0"], p["a1"], p["a2"],
              p["g1"], p["g2"], p["k_k"], p["k_a"], p["r_k"], e, e2, et]
    f32_tile = jax.ShapeDtypeStruct((b, t, d), F32)
    return pl.pallas_call(
        _rwkv_prep_kernel,
        out_shape=[f32_tile] * 7 + [jax.ShapeDtypeStruct((b, t, d), BF16),
                                    jax.ShapeDtypeStruct((b, t, LANES), F32)],
        grid=(b, t // tm),
        in_specs=[tile,
                  pl.BlockSpec((1, SUBLANES, d), lambda bi, i: (bi, jnp.maximum(i * hb - 1, 0), 0)),
                  vec] + [full(a) for a in consts],
        out_specs=[tile] * 8 + [pl.BlockSpec((1, tm, LANES), lambda bi, i: (bi, i, 0))],
        scratch_shapes=[pltpu.VMEM((tm + SUBLANES, d), F32)],
        compiler_params=_cparams("parallel", "parallel"),
        name=name,
    )(x, x, _row2(g), *consts)


def _rwkv_scan_kernel(dec_ref, b_ref, k_ref, v_ref, sc_ref, an_ref, wr_ref, ann_ref, wrn_ref,
                      y_ref, s_ref, sa_ref, ys_ref):
    tb, nk, nh = dec_ref.shape
    nv = s_ref.shape[1]

    @pl.when(pl.program_id(0) == 0)
    def _():
        s_ref[...] = jnp.zeros_like(s_ref)
        sa_ref[...] = jnp.zeros_like(sa_ref)
        ys_ref[...] = jnp.zeros_like(ys_ref)

    def step(t, ln, a_next, wr_next):
        sa, ys, v = sa_ref[:, ln], ys_ref[:, ln], v_ref[t, :, ln]
        y = ys + sa * sc_ref[t, 0:1, ln] + v * sc_ref[t, 1:2, ln]
        yc = y - jnp.sum(y, axis=0, keepdims=True) * (1.0 / nv)
        var = jnp.sum(yc * yc, axis=0, keepdims=True) * (1.0 / nv)
        y_ref[t, :, ln] = yc * lax.rsqrt(var + RWKV_GN_EPS)
        sa_n = jnp.zeros_like(sa)
        ys_n = jnp.zeros_like(sa)
        for j in range(nk):
            s = (s_ref[j, :, ln] * dec_ref[t, j:j + 1, ln]
                 + sa * b_ref[t, j:j + 1, ln] + v * k_ref[t, j:j + 1, ln])
            s_ref[j, :, ln] = s
            sa_n = sa_n + s * a_next(j)
            ys_n = ys_n + s * wr_next(j)
        sa_ref[:, ln] = sa_n
        ys_ref[:, ln] = ys_n

    for hb in range(nh // LANES):
        ln = slice(hb * LANES, (hb + 1) * LANES)

        def body(t, carry, ln=ln):
            step(t, ln, lambda j: an_ref[t + 1, j:j + 1, ln], lambda j: wr_ref[t + 1, j:j + 1, ln])
            return carry

        lax.fori_loop(0, tb - 1, body, 0)
        step(tb - 1, ln, lambda j: ann_ref[0, j:j + 1, ln], lambda j: wrn_ref[0, j:j + 1, ln])


def _rwkv_scan(dec, an, bb, kk, wr, v, sc, *, tb=16, name):
    t, n, nh = dec.shape
    blk = pl.BlockSpec((tb, n, nh), lambda i: (i, 0, 0))
    row = pl.BlockSpec((tb, 2, nh), lambda i: (i, 0, 0))
    nxt = pl.BlockSpec((1, n, nh), lambda i: (jnp.minimum((i + 1) * tb, t - 1), 0, 0))
    return pl.pallas_call(
        _rwkv_scan_kernel,
        out_shape=jax.ShapeDtypeStruct((t, n, nh), F32),
        grid=(t // tb,),
        in_specs=[blk, blk, blk, blk, row, blk, blk, nxt, nxt],
        out_specs=blk,
        scratch_shapes=[pltpu.VMEM((n, n, nh), F32), pltpu.VMEM((n, nh), F32), pltpu.VMEM((n, nh), F32)],
        compiler_params=_cparams("arbitrary"),
        name=name,
    )(dec, bb, kk, v, sc, an, wr, an, wr)


def _rwkv_post_kernel(y_ref, bv_ref, g_ref, x_ref, gg_ref, gb_ref, w_ref, o_ref):
    z = (y_ref[...] * gg_ref[...] + gb_ref[...] + bv_ref[...]) * g_ref[...].astype(F32)
    o_ref[...] = x_ref[...] + _dot(z.astype(BF16), w_ref[...])


def _rwkv_post(y, bv, g, x2d, gn_g, gn_b, w_o, *, tm=1024, name):
    m, d = x2d.shape
    tile = pl.BlockSpec((tm, d), lambda i: (i, 0))
    vec = pl.BlockSpec((1, d), lambda i: (0, 0))
    return pl.pallas_call(
        _rwkv_post_kernel,
        out_shape=jax.ShapeDtypeStruct((m, d), F32),
        grid=(m // tm,),
        in_specs=[tile, tile, tile, tile, vec, vec, pl.BlockSpec((d, d), lambda i: (0, 0))],
        out_specs=tile,
        compiler_params=_cparams("parallel"),
        name=name,
    )(y, bv, g, x2d, _row2(gn_g), _row2(gn_b), w_o)


def _rwkv_mixer(x, g, p):
    b, t, d = x.shape
    n = RWKV_HEAD_DIM
    h = d // n
    lane = jnp.arange(LANES)[None, :]
    head = (jnp.arange(d) // n)[:, None]
    e = (head == lane).astype(BF16)
    e2 = (head + h == lane).astype(BF16)
    et = e.T
    dec, an, bb, kk, wr, v, bv, gate, s = _rwkv_prep(x, g, p, e, e2, et, name="rwkv_prep")

    def tm(a):
        return a.reshape(b, t, h, n).transpose(1, 3, 0, 2).reshape(t, n, b * h)

    sc = s[:, :, :2 * h].reshape(b, t, 2, h).transpose(1, 2, 0, 3).reshape(t, 2, b * h)
    y = _rwkv_scan(tm(dec), tm(an), tm(bb), tm(kk), tm(wr), tm(v), sc, name="rwkv_scan")
    y = y.reshape(t, n, b, h).transpose(2, 0, 3, 1).reshape(b * t, d)
    out = _rwkv_post(y, bv.reshape(b * t, d), gate.reshape(b * t, d), x.reshape(b * t, d),
                     p["gn_g"], p["gn_b"], p["w_o"], name="rwkv_post")
    return out.reshape(b, t, d)


def _sgu_tail_kernel(u_ref, v_ref, lg_ref, lb_ref, ws_ref, bst_ref, w_ref, x_ref, o_ref, vn_ref, gt_ref):
    tm, width = u_ref.shape[1], u_ref.shape[2]
    c = SGU_CHUNK
    gd = width // SGU_GROUPS
    vn_ref[...] = _layernorm(v_ref[0].astype(F32), lg_ref[...], lb_ref[...], LN_EPS).astype(BF16)
    tri = lax.broadcasted_iota(jnp.int32, (c, c), 0) >= lax.broadcasted_iota(jnp.int32, (c, c), 1)
    for gi in range(SGU_GROUPS):
        wsm = jnp.where(tri, ws_ref[gi], 0.0).astype(BF16)
        bias = bst_ref[:, gi:gi + 1]
        cols = slice(gi * gd, (gi + 1) * gd)
        for ci in range(tm // c):
            rows = slice(ci * c, (ci + 1) * c)
            mixed = _dot(wsm, vn_ref[rows, cols]) + bias
            gt_ref[rows, cols] = (u_ref[0, rows, cols].astype(F32) * mixed).astype(BF16)
    o_ref[0] = x_ref[0] + _dot(gt_ref[...], w_ref[...])


def _sgu_tail(uv, ln_g, ln_b, ws, bs, w_out, x, *, tm=512, name):
    b, t, d = x.shape
    width = uv.shape[-1] // 2
    return pl.pallas_call(
        _sgu_tail_kernel,
        out_shape=jax.ShapeDtypeStruct((b, t, d), F32),
        grid=(b, t // tm),
        in_specs=[pl.BlockSpec((1, tm, width), lambda bi, i: (bi, i, 0)),
                  pl.BlockSpec((1, tm, width), lambda bi, i: (bi, i, 1)),
                  pl.BlockSpec((1, width), lambda bi, i: (0, 0)),
                  pl.BlockSpec((1, width), lambda bi, i: (0, 0)),
                  pl.BlockSpec(ws.shape, lambda bi, i: (0, 0, 0)),
                  pl.BlockSpec((SGU_CHUNK, SGU_GROUPS), lambda bi, i: (0, 0)),
                  pl.BlockSpec((width, d), lambda bi, i: (0, 0)),
                  pl.BlockSpec((1, tm, d), lambda bi, i: (bi, i, 0))],
        out_specs=pl.BlockSpec((1, tm, d), lambda bi, i: (bi, i, 0)),
        scratch_shapes=[pltpu.VMEM((tm, width), BF16), pltpu.VMEM((tm, width), BF16)],
        compiler_params=_cparams("parallel", "parallel"),
        name=name,
    )(uv, uv, _row2(ln_g), _row2(ln_b), ws.astype(F32), bs.T.astype(F32), w_out, x)


def _rotate_every_two(x):
    n = x.shape[-1]
    even = lax.broadcasted_iota(jnp.int32, x.shape, 1) % 2 == 0
    return jnp.where(even, -pltpu.roll(x, n - 1, 1), pltpu.roll(x, 1, 1))


def _ret_core_kernel(q_ref, k_ref, v_ref, g_ref, sin_ref, cos_ref, dm_ref, qd_ref, kd_ref, cd_ref,
                     o_ref, r_ref):
    @pl.when(pl.program_id(2) == 0)
    def _():
        r_ref[...] = jnp.zeros_like(r_ref)

    sin, cos = sin_ref[...], cos_ref[...]
    q = q_ref[0].astype(F32)
    k = k_ref[0].astype(F32) * (q.shape[-1] ** -0.5)
    q = q * cos + _rotate_every_two(q) * sin
    k = k * cos + _rotate_every_two(k) * sin
    vb = v_ref[0]
    scores = _dot_nt(q.astype(BF16), k.astype(BF16)) * dm_ref[0]
    inner = _dot(scores.astype(BF16), vb)
    r = r_ref[...]
    cross = _dot((q * qd_ref[0]).astype(BF16), r.astype(BF16))
    ks_t = jnp.transpose(k * kd_ref[0]).astype(BF16)
    r_ref[...] = r * cd_ref[0, 0:1, 0:1] + _dot(ks_t, vb)
    o = inner + cross
    o = o * lax.rsqrt(jnp.mean(o * o, axis=-1, keepdims=True) + NORM_EPS)
    gate = g_ref[0].astype(F32)
    o_ref[0] = (gate * jax.nn.sigmoid(gate) * o).astype(o_ref.dtype)


def _ret_core(qkvg, t, d, *, name):
    b = qkvg.shape[0]
    hh, c = RET_HEADS, RET_CHUNK
    dk = d // hh
    dv = 2 * dk
    angle = jnp.repeat(1.0 / (10000.0 ** jnp.linspace(0.0, 1.0, dk // 2, dtype=F32)), 2)
    theta = jnp.arange(t, dtype=F32)[:, None] * angle[None]
    log_gamma = jnp.log(1.0 - 2.0 ** (-5.0 - jnp.arange(hh, dtype=F32)))
    idx = jnp.arange(c, dtype=F32)
    diff = idx[:, None] - idx[None, :]
    dm = jnp.where(diff[None] >= 0, jnp.exp(log_gamma[:, None, None] * jnp.maximum(diff, 0.0)[None]), 0.0)
    qd = jnp.exp(log_gamma[:, None] * (idx + 1.0))[:, :, None]
    kd = jnp.exp(log_gamma[:, None] * (c - 1.0 - idx))[:, :, None]
    cd = jnp.broadcast_to(jnp.exp(log_gamma * c)[:, None, None], (hh, SUBLANES, LANES))
    nq = d // dk
    nv = 2 * d // dv
    return pl.pallas_call(
        _ret_core_kernel,
        out_shape=jax.ShapeDtypeStruct((b, t, hh * dv), BF16),
        grid=(b, hh, t // c),
        in_specs=[pl.BlockSpec((1, c, dk), lambda bi, h, n: (bi, n, h)),
                  pl.BlockSpec((1, c, dk), lambda bi, h, n: (bi, n, nq + h)),
                  pl.BlockSpec((1, c, dv), lambda bi, h, n: (bi, n, nv + h)),
                  pl.BlockSpec((1, c, dv), lambda bi, h, n: (bi, n, nv + hh + h)),
                  pl.BlockSpec((c, dk), lambda bi, h, n: (n, 0)),
                  pl.BlockSpec((c, dk), lambda bi, h, n: (n, 0)),
                  pl.BlockSpec((1, c, c), lambda bi, h, n: (h, 0, 0)),
                  pl.BlockSpec((1, c, 1), lambda bi, h, n: (h, 0, 0)),
                  pl.BlockSpec((1, c, 1), lambda bi, h, n: (h, 0, 0)),
                  pl.BlockSpec((1, SUBLANES, LANES), lambda bi, h, n: (h, 0, 0))],
        out_specs=pl.BlockSpec((1, c, dv), lambda bi, h, n: (bi, n, h)),
        scratch_shapes=[pltpu.VMEM((dk, dv), F32)],
        compiler_params=_cparams("parallel", "parallel", "arbitrary"),
        name=name,
    )(qkvg, qkvg, qkvg, qkvg, jnp.sin(theta), jnp.cos(theta), dm, qd, kd, cd)


def _pad_axis(a, axis, to):
    pad = [(0, 0)] * a.ndim
    pad[axis] = (0, to - a.shape[axis])
    return jnp.pad(a, pad)


def kernel(x, mem, mix_norm_g, xattn_norm_g, mem_norm_g, xattn_wq, xattn_wkv, xattn_wo, ffn_norm_g, ffn_w_gate_up, ffn_w_down, conv_w_in, conv_b_in, conv_dw, conv_dw_b, conv_ln_g, conv_ln_b, conv_w_out, conv_b_out, rwkv_mix, rwkv_w_rkv, rwkv_w0, rwkv_w1, rwkv_w2, rwkv_a0, rwkv_a1, rwkv_a2, rwkv_g1, rwkv_g2, rwkv_k_k, rwkv_k_a, rwkv_r_k, rwkv_gn_g, rwkv_gn_b, rwkv_w_o, sgu_w_in, sgu_b_in, sgu_ln_g, sgu_ln_b, sgu_ws, sgu_bs, sgu_w_out, ret_w_in, ret_w_out, final_norm_g):
    b, t, d = x.shape
    n_mem = mem.shape[1]
    depth = mix_norm_g.shape[0]
    m = b * t
    mem2d = mem.reshape(b * n_mem, d)

    for i in range(depth):
        mixer, j = i % 4, i // 4
        g = mix_norm_g[i]
        if mixer == 0:
            z = _norm_mm(x.reshape(m, d), g, conv_w_in[j].astype(BF16), conv_b_in[j], glu=True,
                         name="conv_in")
            x = _conv_tail(z.reshape(b, t, d), conv_dw[j], conv_dw_b[j], conv_ln_g[j], conv_ln_b[j],
                           conv_w_out[j].astype(BF16), conv_b_out[j], x, name="conv_tail")
        elif mixer == 1:
            lw = _pad_axis(rwkv_w1[j], 1, LANES)
            la = _pad_axis(rwkv_a1[j], 1, LANES)
            lg = _pad_axis(rwkv_g1[j], 1, 2 * LANES)
            p = dict(mix=rwkv_mix[j].astype(F32),
                     wr=rwkv_w_rkv[j, 0].astype(BF16), wk=rwkv_w_rkv[j, 1].astype(BF16),
                     wv=rwkv_w_rkv[j, 2].astype(BF16),
                     w0=_row2(rwkv_w0[j]), w1=lw.astype(BF16),
                     w2=_pad_axis(rwkv_w2[j], 0, LANES).astype(BF16),
                     a0=_row2(rwkv_a0[j]), a1=la.astype(BF16),
                     a2=_pad_axis(rwkv_a2[j], 0, LANES).astype(BF16),
                     g1=lg.astype(BF16), g2=_pad_axis(rwkv_g2[j], 0, 2 * LANES).astype(BF16),
                     k_k=_row2(rwkv_k_k[j]), k_a=_row2(rwkv_k_a[j]), r_k=_row2(rwkv_r_k[j]),
                     gn_g=rwkv_gn_g[j], gn_b=rwkv_gn_b[j], w_o=rwkv_w_o[j].astype(BF16))
            x = _rwkv_mixer(x, g, p)
        elif mixer == 2:
            uv = _norm_mm(x.reshape(m, d), g, sgu_w_in[j].astype(BF16), sgu_b_in[j], act="gelu",
                          name="sgu_in")
            x = _sgu_tail(uv.reshape(b, t, -1), sgu_ln_g[j], sgu_ln_b[j], sgu_ws[j], sgu_bs[j],
                          sgu_w_out[j].astype(BF16), x, name="sgu_tail")
        else:
            qkvg = _norm_mm(x.reshape(m, d), g, ret_w_in[j].astype(BF16), name="ret_in")
            o = _ret_core(qkvg.reshape(b, t, -1), t, d, name="ret_core")
            x = _mm_res(o.reshape(m, -1), ret_w_out[j].astype(BF16), x.reshape(m, d),
                        name="ret_out").reshape(b, t, d)

        kv = _norm_mm(mem2d, mem_norm_g[i], xattn_wkv[i].astype(BF16), name="xattn_kv")
        x = _xattn(x, xattn_norm_g[i], xattn_wq[i].astype(BF16), kv.reshape(b, n_mem, 2 * d),
                   xattn_wo[i].astype(BF16), name="xattn")
        fg = final_norm_g if i == depth - 1 else None
        x = _swiglu(x.reshape(m, d), ffn_norm_g[i], ffn_w_gate_up[i].astype(BF16),
                    ffn_w_down[i].astype(BF16), fg, name="swiglu").reshape(b, t, d)
    return x
```

```python
import functools
import math

import jax
import jax.numpy as jnp
from jax import lax
from jax.experimental import pallas as pl
from jax.experimental.pallas import tpu as pltpu

F32 = jnp.float32
BF16 = jnp.bfloat16

NORM_EPS = 1e-6
LN_EPS = 1e-5

XA_HEADS = 4
CONV_WIDTH = 31
CONV_HALO = 32
RWKV_HEAD_DIM = 64
RWKV_GN_EPS = RWKV_HEAD_DIM * 1e-5
SGU_CHUNK = 128
SGU_GROUPS = 8
RET_HEADS = 4
RET_CHUNK = 512

LANES = 128
SUBLANES = 8
VMEM_LIMIT_BYTES = 56 * 1024 * 1024


def _cparams(*sem):
    return pltpu.CompilerParams(dimension_semantics=sem, vmem_limit_bytes=VMEM_LIMIT_BYTES)


def _rms(x, g):
    return x * lax.rsqrt(jnp.mean(x * x, axis=-1, keepdims=True) + NORM_EPS) * g


def _layernorm(x, g, b, eps):
    mu = jnp.mean(x, axis=-1, keepdims=True)
    xc = x - mu
    var = jnp.mean(xc * xc, axis=-1, keepdims=True)
    return xc * lax.rsqrt(var + eps) * g + b


def _dot(a, b):
    return jnp.dot(a, b, preferred_element_type=F32)


def _dot_nt(a, b):
    return lax.dot_general(a, b, (((1,), (1,)), ((), ())), preferred_element_type=F32)


def _head_sum(x, e):
    return _dot(x.astype(BF16), e)


def _row2(v):
    return v.reshape(1, -1).astype(F32)


def _norm_mm_kernel(*refs, act, glu, has_bias):
    it = iter(refs)
    x_ref, g_ref, w_ref = next(it), next(it), next(it)
    w2_ref = next(it) if glu else None
    b_ref = next(it) if has_bias else None
    b2_ref = next(it) if (glu and has_bias) else None
    o_ref, xn_ref = next(it), next(it)

    @pl.when(pl.program_id(1) == 0)
    def _():
        xn_ref[...] = _rms(x_ref[...], g_ref[...]).astype(BF16)

    xn = xn_ref[...]
    y = _dot(xn, w_ref[...])
    if has_bias:
        y = y + b_ref[...]
    if glu:
        y2 = _dot(xn, w2_ref[...])
        if has_bias:
            y2 = y2 + b2_ref[...]
        y = y * jax.nn.sigmoid(y2)
    elif act == "gelu":
        y = 0.5 * y * (1.0 + lax.erf(y * math.sqrt(0.5)))
    o_ref[...] = y.astype(o_ref.dtype)


def _norm_mm(x2d, g, w, bias=None, *, act=None, glu=False, out_dtype=BF16, tm=1024, tn=1024, name):
    m, k = x2d.shape
    n_w = w.shape[1]
    n_out = n_w // 2 if glu else n_w
    tm, tn = min(tm, m), min(tn, n_out)
    assert m % tm == 0 and n_out % tn == 0
    nb = n_out // tn
    has_bias = bias is not None
    args = [x2d, _row2(g), w]
    specs = [pl.BlockSpec((tm, k), lambda i, j: (i, 0)),
             pl.BlockSpec((1, k), lambda i, j: (0, 0)),
             pl.BlockSpec((k, tn), lambda i, j: (0, j))]
    if glu:
        args.append(w)
        specs.append(pl.BlockSpec((k, tn), lambda i, j: (0, j + nb)))
    if has_bias:
        b2d = _row2(bias)
        args.append(b2d)
        specs.append(pl.BlockSpec((1, tn), lambda i, j: (0, j)))
        if glu:
            args.append(b2d)
            specs.append(pl.BlockSpec((1, tn), lambda i, j: (0, j + nb)))
    return pl.pallas_call(
        functools.partial(_norm_mm_kernel, act=act, glu=glu, has_bias=has_bias),
        out_shape=jax.ShapeDtypeStruct((m, n_out), out_dtype),
        grid=(m // tm, nb),
        in_specs=specs,
        out_specs=pl.BlockSpec((tm, tn), lambda i, j: (i, j)),
        scratch_shapes=[pltpu.VMEM((tm, k), BF16)],
        compiler_params=_cparams("parallel", "arbitrary"),
        name=name,
    )(*args)


def _mm_res_kernel(a_ref, w_ref, r_ref, o_ref):
    o_ref[...] = r_ref[...] + _dot(a_ref[...], w_ref[...])


def _mm_res(a, w, res, *, tm=1024, name):
    m, k = a.shape
    n = w.shape[1]
    assert m % tm == 0
    return pl.pallas_call(
        _mm_res_kernel,
        out_shape=jax.ShapeDtypeStruct((m, n), F32),
        grid=(m // tm,),
        in_specs=[pl.BlockSpec((tm, k), lambda i: (i, 0)),
                  pl.BlockSpec((k, n), lambda i: (0, 0)),
                  pl.BlockSpec((tm, n), lambda i: (i, 0))],
        out_specs=pl.BlockSpec((tm, n), lambda i: (i, 0)),
        compiler_params=_cparams("parallel"),
        name=name,
    )(a, w, res)


def _xattn_kernel(x_ref, g_ref, wq_ref, kv_ref, wo_ref, o_ref, oh_ref, *, heads):
    x = x_ref[0]
    d = x.shape[-1]
    hd = d // heads
    xn = _rms(x, g_ref[...]).astype(BF16)
    q = _dot(xn, wq_ref[...])
    for h in range(heads):
        qh = q[:, h * hd:(h + 1) * hd].astype(BF16)
        kh = kv_ref[0, :, h * hd:(h + 1) * hd]
        vh = kv_ref[0, :, d + h * hd:d + (h + 1) * hd]
        s = _dot_nt(qh, kh) * (hd ** -0.5)
        p = jnp.exp(s - jnp.max(s, axis=-1, keepdims=True))
        p = p / jnp.sum(p, axis=-1, keepdims=True)
        oh_ref[:, h * hd:(h + 1) * hd] = _dot(p.astype(BF16), vh).astype(BF16)
    o_ref[0] = x + _dot(oh_ref[...], wo_ref[...])


def _xattn(x, g, wq, kv, wo, *, tm=512, name):
    b, t, d = x.shape
    m = kv.shape[1]
    return pl.pallas_call(
        functools.partial(_xattn_kernel, heads=XA_HEADS),
        out_shape=jax.ShapeDtypeStruct((b, t, d), F32),
        grid=(b, t // tm),
        in_specs=[pl.BlockSpec((1, tm, d), lambda bi, i: (bi, i, 0)),
                  pl.BlockSpec((1, d), lambda bi, i: (0, 0)),
                  pl.BlockSpec((d, d), lambda bi, i: (0, 0)),
                  pl.BlockSpec((1, m, 2 * d), lambda bi, i: (bi, 0, 0)),
                  pl.BlockSpec((d, d), lambda bi, i: (0, 0))],
        out_specs=pl.BlockSpec((1, tm, d), lambda bi, i: (bi, i, 0)),
        scratch_shapes=[pltpu.VMEM((tm, d), BF16)],
        compiler_params=_cparams("parallel", "parallel"),
        name=name,
    )(x, _row2(g), wq, kv, wo)


def _swiglu_kernel(*refs, final):
    if final:
        x_ref, g_ref, wg_ref, wu_ref, wd_ref, fg_ref, o_ref, xn_ref, acc_ref = refs
    else:
        x_ref, g_ref, wg_ref, wu_ref, wd_ref, o_ref, xn_ref, acc_ref = refs
    f = pl.program_id(1)

    @pl.when(f == 0)
    def _():
        xn_ref[...] = _rms(x_ref[...], g_ref[...]).astype(BF16)
        acc_ref[...] = jnp.zeros_like(acc_ref)

    xn = xn_ref[...]
    gt = _dot(xn, wg_ref[...])
    up = _dot(xn, wu_ref[...])
    h = (gt * jax.nn.sigmoid(gt) * up).astype(BF16)
    acc_ref[...] += _dot(h, wd_ref[...])

    @pl.when(f == pl.num_programs(1) - 1)
    def _():
        y = x_ref[...] + acc_ref[...]
        if final:
            y = _rms(y, fg_ref[...])
        o_ref[...] = y


def _swiglu(x2d, g, w_gate_up, w_down, final_g=None, *, tm=512, name):
    m, d = x2d.shape
    ff = w_down.shape[0]
    nf = 2
    tf = ff // nf
    assert tf % LANES == 0 and m % tm == 0
    final = final_g is not None
    args = [x2d, _row2(g), w_gate_up, w_gate_up, w_down]
    specs = [pl.BlockSpec((tm, d), lambda i, f: (i, 0)),
             pl.BlockSpec((1, d), lambda i, f: (0, 0)),
             pl.BlockSpec((d, tf), lambda i, f: (0, f)),
             pl.BlockSpec((d, tf), lambda i, f: (0, f + nf)),
             pl.BlockSpec((tf, d), lambda i, f: (f, 0))]
    if final:
        args.append(_row2(final_g))
        specs.append(pl.BlockSpec((1, d), lambda i, f: (0, 0)))
    return pl.pallas_call(
        functools.partial(_swiglu_kernel, final=final),
        out_shape=jax.ShapeDtypeStruct((m, d), F32),
        grid=(m // tm, nf),
        in_specs=specs,
        out_specs=pl.BlockSpec((tm, d), lambda i, f: (i, 0)),
        scratch_shapes=[pltpu.VMEM((tm, d), BF16), pltpu.VMEM((tm, d), F32)],
        compiler_params=_cparams("parallel", "arbitrary"),
        name=name,
    )(*args)


def _conv_tail_kernel(z_ref, zh_ref, dw_ref, dwb_ref, lg_ref, lb_ref, w_ref, b_ref, x_ref,
                      o_ref, zz_ref, sh_ref, c_ref, dwx_ref, *, rows):
    tm, d = z_ref.shape[1], z_ref.shape[2]
    not_first = (pl.program_id(1) != 0).astype(F32)
    zz_ref[0:CONV_HALO, :] = zh_ref[0].astype(F32) * not_first
    zz_ref[CONV_HALO:, :] = z_ref[0].astype(F32)
    for s in range(1, SUBLANES):
        sh_ref[s - 1] = zz_ref[pl.ds(s, sh_ref.shape[1]), :]
    for k in range(CONV_WIDTH):
        dwx_ref[k] = jnp.broadcast_to(dw_ref[k:k + 1, :], (SUBLANES, d))
    dwx_ref[CONV_WIDTH] = jnp.broadcast_to(dwb_ref[...], (SUBLANES, d))
    base = CONV_HALO - (CONV_WIDTH - 1)
    groups = rows // SUBLANES

    def chunk(c, carry):
        r0 = pl.multiple_of(c * rows, rows)
        acc = [dwx_ref[CONV_WIDTH]] * groups
        for k in range(CONV_WIDTH):
            blk, s = divmod(base + k, SUBLANES)
            src = zz_ref if s == 0 else sh_ref.at[s - 1]
            wk = dwx_ref[k]
            for gi in range(groups):
                acc[gi] = acc[gi] + src[pl.ds(r0 + (blk + gi) * SUBLANES, SUBLANES), :] * wk
        for gi in range(groups):
            c_ref[pl.ds(r0 + gi * SUBLANES, SUBLANES), :] = acc[gi]
        return carry

    lax.fori_loop(0, tm // rows, chunk, 0)
    y = _layernorm(c_ref[...], lg_ref[...], lb_ref[...], LN_EPS)
    act = (y * jax.nn.sigmoid(y)).astype(BF16)
    o_ref[0] = x_ref[0] + _dot(act, w_ref[...]) + b_ref[...]


def _conv_tail(z, dw, dw_b, ln_g, ln_b, w_out, b_out, x, *, tm=512, rows=32, name):
    b, t, d = x.shape
    hb = tm // CONV_HALO
    vec = pl.BlockSpec((1, d), lambda bi, i: (0, 0))
    return pl.pallas_call(
        functools.partial(_conv_tail_kernel, rows=rows),
        out_shape=jax.ShapeDtypeStruct((b, t, d), F32),
        grid=(b, t // tm),
        in_specs=[pl.BlockSpec((1, tm, d), lambda bi, i: (bi, i, 0)),
                  pl.BlockSpec((1, CONV_HALO, d), lambda bi, i: (bi, jnp.maximum(i * hb - 1, 0), 0)),
                  pl.BlockSpec((CONV_WIDTH, d), lambda bi, i: (0, 0)),
                  vec, vec, vec,
                  pl.BlockSpec((d, d), lambda bi, i: (0, 0)),
                  vec,
                  pl.BlockSpec((1, tm, d), lambda bi, i: (bi, i, 0))],
        out_specs=pl.BlockSpec((1, tm, d), lambda bi, i: (bi, i, 0)),
        scratch_shapes=[pltpu.VMEM((tm + CONV_HALO, d), F32),
                        pltpu.VMEM((SUBLANES - 1, tm + CONV_HALO - SUBLANES, d), F32),
                        pltpu.VMEM((tm, d), F32),
                        pltpu.VMEM((CONV_WIDTH + 1, SUBLANES, d), F32)],
        compiler_params=_cparams("parallel", "parallel"),
        name=name,
    )(z, z, dw.astype(F32), _row2(dw_b), _row2(ln_g), _row2(ln_b), w_out, _row2(b_out), x)


def _softplus(x):
    return jnp.maximum(x, 0.0) + jnp.log(1.0 + jnp.exp(-jnp.abs(x)))


def _rwkv_prep_kernel(x_ref, xh_ref, g_ref, mix_ref, wr_ref, wk_ref, wv_ref, w0_ref, w1_ref, w2_ref,
                      a0_ref, a1_ref, a2_ref, g1_ref, g2_ref, kk_ref, ka_ref, rk_ref, e_ref, e2_ref, et_ref,
                      dec_o, an_o, b_o, k_o, wr_o, v_o, bv_o, g_o, s_o, hs_ref):
    tm = x_ref.shape[1]
    gn = g_ref[...]
    hn = _rms(x_ref[0], gn)
    not_first = (pl.program_id(1) != 0).astype(F32)
    hs_ref[0:SUBLANES, :] = _rms(xh_ref[0], gn) * not_first
    hs_ref[SUBLANES:, :] = hn
    delta = hs_ref[pl.ds(SUBLANES - 1, tm), :] - hn

    def mixed(c):
        return (hn + delta * mix_ref[c:c + 1, :]).astype(BF16)

    r = _dot(mixed(0), wr_ref[...])
    k = _dot(mixed(2), wk_ref[...])
    v = _dot(mixed(3), wv_ref[...])
    w_lora = _dot(jnp.tanh(_dot(mixed(1), w1_ref[...])).astype(BF16), w2_ref[...])
    w = -_softplus(-(w0_ref[...] + w_lora)) - 0.5
    decay = jnp.exp(-jnp.exp(w))
    a = jax.nn.sigmoid(a0_ref[...] + _dot(_dot(mixed(4), a1_ref[...]).astype(BF16), a2_ref[...]))
    gate = _dot(jax.nn.sigmoid(_dot(mixed(5), g1_ref[...])).astype(BF16), g2_ref[...])

    e, e2, et = e_ref[...], e2_ref[...], et_ref[...]
    kk = k * kk_ref[...]
    ss = _dot(_head_sum(kk * kk, e).astype(BF16), et)
    kk = kk / jnp.maximum(jnp.sqrt(ss), 1e-12)
    k = k * (1.0 + (a - 1.0) * ka_ref[...])
    b = kk * a
    bonus = _dot(_head_sum(r * k * rk_ref[...], e).astype(BF16), et)

    dec_o[0] = decay
    an_o[0] = (-kk).astype(an_o.dtype)
    b_o[0] = b.astype(b_o.dtype)
    k_o[0] = k.astype(k_o.dtype)
    wr_o[0] = (decay * r).astype(wr_o.dtype)
    v_o[0] = v.astype(v_o.dtype)
    bv_o[0] = (bonus * v).astype(bv_o.dtype)
    g_o[0] = gate.astype(g_o.dtype)
    s_o[0] = _head_sum(b * r, e) + _head_sum(k * r, e2)


def _rwkv_prep(x, g, p, e, e2, et, *, tm=256, name):
    b, t, d = x.shape
    hb = tm // SUBLANES
    tile = pl.BlockSpec((1, tm, d), lambda bi, i: (bi, i, 0))
    vec = pl.BlockSpec((1, d), lambda bi, i: (0, 0))

    def full(a):
        return pl.BlockSpec(a.shape, lambda bi, i: (0,) * a.ndim)

    consts = [p["mix"], p["wr"], p["wk"], p["wv"], p["w0"], p["w1"], p["w2"], p["a0"], p["a1"], p["a2"],
              p["g1"], p["g2"], p["k_k"], p["k_a"], p["r_k"], e, e2, et]
    f32_tile = jax.ShapeDtypeStruct((b, t, d), F32)
    bf16_tile = jax.ShapeDtypeStruct((b, t, d), BF16)
    return pl.pallas_call(
        _rwkv_prep_kernel,
        out_shape=[f32_tile] + [bf16_tile] * 7 + [jax.ShapeDtypeStruct((b, t, LANES), F32)],
        grid=(b, t // tm),
        in_specs=[tile,
                  pl.BlockSpec((1, SUBLANES, d), lambda bi, i: (bi, jnp.maximum(i * hb - 1, 0), 0)),
                  vec] + [full(a) for a in consts],
        out_specs=[tile] * 8 + [pl.BlockSpec((1, tm, LANES), lambda bi, i: (bi, i, 0))],
        scratch_shapes=[pltpu.VMEM((tm + SUBLANES, d), F32)],
        compiler_params=_cparams("parallel", "parallel"),
        name=name,
    )(x, x, _row2(g), *consts)


def _rwkv_scan_kernel(dec_ref, b_ref, k_ref, v_ref, sc_ref, an_ref, wr_ref, ann_ref, wrn_ref,
                      y_ref, s_ref, sa_ref, ys_ref, rows_ref):
    tb, nk, nh = dec_ref.shape
    nv = s_ref.shape[1]

    @pl.when(pl.program_id(0) == 0)
    def _():
        s_ref[...] = jnp.zeros_like(s_ref)
        sa_ref[...] = jnp.zeros_like(sa_ref)
        ys_ref[...] = jnp.zeros_like(ys_ref)

    def step(t, ln, a_next, wr_next):
        sa, ys, v = sa_ref[:, ln], ys_ref[:, ln], v_ref[t, :, ln].astype(F32)
        y = ys + sa * sc_ref[t, 0:1, ln] + v * sc_ref[t, 1:2, ln]
        yc = y - jnp.sum(y, axis=0, keepdims=True) * (1.0 / nv)
        var = jnp.sum(yc * yc, axis=0, keepdims=True) * (1.0 / nv)
        y_ref[t, :, ln] = (yc * lax.rsqrt(var + RWKV_GN_EPS)).astype(y_ref.dtype)
        rows_ref[0] = b_ref[t, :, ln].astype(F32)
        rows_ref[1] = k_ref[t, :, ln].astype(F32)
        rows_ref[2] = a_next.astype(F32)
        rows_ref[3] = wr_next.astype(F32)
        sa_n = jnp.zeros_like(sa)
        ys_n = jnp.zeros_like(sa)
        for j in range(nk):
            s = (s_ref[j, :, ln] * dec_ref[t, j:j + 1, ln]
                 + sa * rows_ref[0, j:j + 1, :] + v * rows_ref[1, j:j + 1, :])
            s_ref[j, :, ln] = s
            sa_n = sa_n + s * rows_ref[2, j:j + 1, :]
            ys_n = ys_n + s * rows_ref[3, j:j + 1, :]
        sa_ref[:, ln] = sa_n
        ys_ref[:, ln] = ys_n

    for hb in range(nh // LANES):
        ln = slice(hb * LANES, (hb + 1) * LANES)

        def body(t, carry, ln=ln):
            step(t, ln, an_ref[t + 1, :, ln], wr_ref[t + 1, :, ln])
            return carry

        lax.fori_loop(0, tb - 1, body, 0)
        step(tb - 1, ln, ann_ref[0, :, ln], wrn_ref[0, :, ln])


def _rwkv_scan(dec, an, bb, kk, wr, v, sc, *, tb=16, name):
    t, n, nh = dec.shape
    blk = pl.BlockSpec((tb, n, nh), lambda i: (i, 0, 0))
    row = pl.BlockSpec((tb, 2, nh), lambda i: (i, 0, 0))
    nxt = pl.BlockSpec((1, n, nh), lambda i: (jnp.minimum((i + 1) * tb, t - 1), 0, 0))
    return pl.pallas_call(
        _rwkv_scan_kernel,
        out_shape=jax.ShapeDtypeStruct((t, n, nh), BF16),
        grid=(t // tb,),
        in_specs=[blk, blk, blk, blk, row, blk, blk, nxt, nxt],
        out_specs=blk,
        scratch_shapes=[pltpu.VMEM((n, n, nh), F32), pltpu.VMEM((n, nh), F32), pltpu.VMEM((n, nh), F32),
                        pltpu.VMEM((4, n, LANES), F32)],
        compiler_params=_cparams("arbitrary"),
        name=name,
    )(dec, bb, kk, v, sc, an, wr, an, wr)


def _rwkv_post_kernel(y_ref, bv_ref, g_ref, x_ref, gg_ref, gb_ref, w_ref, o_ref):
    z = ((y_ref[...].astype(F32) * gg_ref[...] + gb_ref[...] + bv_ref[...].astype(F32))
         * g_ref[...].astype(F32))
    o_ref[...] = x_ref[...] + _dot(z.astype(BF16), w_ref[...])


def _rwkv_post(y, bv, g, x2d, gn_g, gn_b, w_o, *, tm=1024, name):
    m, d = x2d.shape
    tile = pl.BlockSpec((tm, d), lambda i: (i, 0))
    vec = pl.BlockSpec((1, d), lambda i: (0, 0))
    return pl.pallas_call(
        _rwkv_post_kernel,
        out_shape=jax.ShapeDtypeStruct((m, d), F32),
        grid=(m // tm,),
        in_specs=[tile, tile, tile, tile, vec, vec, pl.BlockSpec((d, d), lambda i: (0, 0))],
        out_specs=tile,
        compiler_params=_cparams("parallel"),
        name=name,
    )(y, bv, g, x2d, _row2(gn_g), _row2(gn_b), w_o)


def _rwkv_mixer(x, g, p):
    b, t, d = x.shape
    n = RWKV_HEAD_DIM
    h = d // n
    lane = jnp.arange(LANES)[None, :]
    head = (jnp.arange(d) // n)[:, None]
    e = (head == lane).astype(BF16)
    e2 = (head + h == lane).astype(BF16)
    et = e.T
    dec, an, bb, kk, wr, v, bv, gate, s = _rwkv_prep(x, g, p, e, e2, et, name="rwkv_prep")

    def tm(a):
        return a.reshape(b, t, h, n).transpose(1, 3, 0, 2).reshape(t, n, b * h)

    sc = s[:, :, :2 * h].reshape(b, t, 2, h).transpose(1, 2, 0, 3).reshape(t, 2, b * h)
    y = _rwkv_scan(tm(dec), tm(an), tm(bb), tm(kk), tm(wr), tm(v), sc, name="rwkv_scan")
    y = y.reshape(t, n, b, h).transpose(2, 0, 3, 1).reshape(b * t, d)
    out = _rwkv_post(y, bv.reshape(b * t, d), gate.reshape(b * t, d), x.reshape(b * t, d),
                     p["gn_g"], p["gn_b"], p["w_o"], name="rwkv_post")
    return out.reshape(b, t, d)


def _sgu_tail_kernel(u_ref, v_ref, lg_ref, lb_ref, ws_ref, bst_ref, w_ref, x_ref, o_ref, vn_ref, gt_ref):
    tm, width = u_ref.shape[1], u_ref.shape[2]
    c = SGU_CHUNK
    gd = width // SGU_GROUPS
    vn_ref[...] = _layernorm(v_ref[0].astype(F32), lg_ref[...], lb_ref[...], LN_EPS).astype(BF16)
    tri = lax.broadcasted_iota(jnp.int32, (c, c), 0) >= lax.broadcasted_iota(jnp.int32, (c, c), 1)
    for gi in range(SGU_GROUPS):
        wsm = jnp.where(tri, ws_ref[gi], 0.0).astype(BF16)
        bias = bst_ref[:, gi:gi + 1]
        cols = slice(gi * gd, (gi + 1) * gd)
        for ci in range(tm // c):
            rows = slice(ci * c, (ci + 1) * c)
            mixed = _dot(wsm, vn_ref[rows, cols]) + bias
            gt_ref[rows, cols] = (u_ref[0, rows, cols].astype(F32) * mixed).astype(BF16)
    o_ref[0] = x_ref[0] + _dot(gt_ref[...], w_ref[...])


def _sgu_tail(uv, ln_g, ln_b, ws, bs, w_out, x, *, tm=512, name):
    b, t, d = x.shape
    width = uv.shape[-1] // 2
    return pl.pallas_call(
        _sgu_tail_kernel,
        out_shape=jax.ShapeDtypeStruct((b, t, d), F32),
        grid=(b, t // tm),
        in_specs=[pl.BlockSpec((1, tm, width), lambda bi, i: (bi, i, 0)),
                  pl.BlockSpec((1, tm, width), lambda bi, i: (bi, i, 1)),
                  pl.BlockSpec((1, width), lambda bi, i: (0, 0)),
                  pl.BlockSpec((1, width), lambda bi, i: (0, 0)),
                  pl.BlockSpec(ws.shape, lambda bi, i: (0, 0, 0)),
                  pl.BlockSpec((SGU_CHUNK, SGU_GROUPS), lambda bi, i: (0, 0)),
                  pl.BlockSpec((width, d), lambda bi, i: (0, 0)),
                  pl.BlockSpec((1, tm, d), lambda bi, i: (bi, i, 0))],
        out_specs=pl.BlockSpec((1, tm, d), lambda bi, i: (bi, i, 0)),
        scratch_shapes=[pltpu.VMEM((tm, width), BF16), pltpu.VMEM((tm, width), BF16)],
        compiler_params=_cparams("parallel", "parallel"),
        name=name,
    )(uv, uv, _row2(ln_g), _row2(ln_b), ws.astype(F32), bs.T.astype(F32), w_out, x)


def _ret_core_kernel(q_ref, k_ref, v_ref, g_ref, sin_ref, cos_ref, rot_ref, dm_ref, qd_ref, kd_ref, cd_ref,
                     o_ref, r_ref):
    @pl.when(pl.program_id(2) == 0)
    def _():
        r_ref[...] = jnp.zeros_like(r_ref)

    sin, cos, rot = sin_ref[...], cos_ref[...], rot_ref[...]
    qb, kb, vb = q_ref[0], k_ref[0], v_ref[0]
    q = qb.astype(F32) * cos + _dot(qb, rot) * sin
    k = (kb.astype(F32) * cos + _dot(kb, rot) * sin) * (qb.shape[-1] ** -0.5)
    scores = _dot_nt(q.astype(BF16), k.astype(BF16)) * dm_ref[0]
    inner = _dot(scores.astype(BF16), vb)
    r = r_ref[...]
    cross = _dot((q * qd_ref[0]).astype(BF16), r.astype(BF16))
    ks = (k * kd_ref[0]).astype(BF16)
    kv = lax.dot_general(ks, vb, (((0,), (0,)), ((), ())), preferred_element_type=F32)
    r_ref[...] = r * cd_ref[0, 0:1, 0:1] + kv
    o = inner + cross
    o = o * lax.rsqrt(jnp.mean(o * o, axis=-1, keepdims=True) + NORM_EPS)
    gate = g_ref[0].astype(F32)
    o_ref[0] = (gate * jax.nn.sigmoid(gate) * o).astype(o_ref.dtype)


def _ret_core(qkvg, t, d, *, name):
    b = qkvg.shape[0]
    hh, c = RET_HEADS, RET_CHUNK
    dk = d // hh
    dv = 2 * dk
    angle = jnp.repeat(1.0 / (10000.0 ** jnp.linspace(0.0, 1.0, dk // 2, dtype=F32)), 2)
    theta = jnp.arange(t, dtype=F32)[:, None] * angle[None]
    log_gamma = jnp.log(1.0 - 2.0 ** (-5.0 - jnp.arange(hh, dtype=F32)))
    idx = jnp.arange(c, dtype=F32)
    diff = idx[:, None] - idx[None, :]
    dm = jnp.where(diff[None] >= 0, jnp.exp(log_gamma[:, None, None] * jnp.maximum(diff, 0.0)[None]), 0.0)
    qd = jnp.exp(log_gamma[:, None] * (idx + 1.0))[:, :, None]
    kd = jnp.exp(log_gamma[:, None] * (c - 1.0 - idx))[:, :, None]
    cd = jnp.broadcast_to(jnp.exp(log_gamma * c)[:, None, None], (hh, SUBLANES, LANES))
    src, dst = jnp.arange(dk)[:, None], jnp.arange(dk)[None, :]
    rot = (jnp.where((dst % 2 == 1) & (src == dst - 1), 1.0, 0.0)
           - jnp.where((dst % 2 == 0) & (src == dst + 1), 1.0, 0.0)).astype(BF16)
    nq = d // dk
    nv = 2 * d // dv
    return pl.pallas_call(
        _ret_core_kernel,
        out_shape=jax.ShapeDtypeStruct((b, t, hh * dv), BF16),
        grid=(b, hh, t // c),
        in_specs=[pl.BlockSpec((1, c, dk), lambda bi, h, n: (bi, n, h)),
                  pl.BlockSpec((1, c, dk), lambda bi, h, n: (bi, n, nq + h)),
                  pl.BlockSpec((1, c, dv), lambda bi, h, n: (bi, n, nv + h)),
                  pl.BlockSpec((1, c, dv), lambda bi, h, n: (bi, n, nv + hh + h)),
                  pl.BlockSpec((c, dk), lambda bi, h, n: (n, 0)),
                  pl.BlockSpec((c, dk), lambda bi, h, n: (n, 0)),
                  pl.BlockSpec((dk, dk), lambda bi, h, n: (0, 0)),
                  pl.BlockSpec((1, c, c), lambda bi, h, n: (h, 0, 0)),
                  pl.BlockSpec((1, c, 1), lambda bi, h, n: (h, 0, 0)),
                  pl.BlockSpec((1, c, 1), lambda bi, h, n: (h, 0, 0)),
                  pl.BlockSpec((1, SUBLANES, LANES), lambda bi, h, n: (h, 0, 0))],
        out_specs=pl.BlockSpec((1, c, dv), lambda bi, h, n: (bi, n, h)),
        scratch_shapes=[pltpu.VMEM((dk, dv), F32)],
        compiler_params=_cparams("parallel", "parallel", "arbitrary"),
        name=name,
    )(qkvg, qkvg, qkvg, qkvg, jnp.sin(theta), jnp.cos(theta), rot, dm, qd, kd, cd)


def _pad_axis(a, axis, to):
    pad = [(0, 0)] * a.ndim
    pad[axis] = (0, to - a.shape[axis])
    return jnp.pad(a, pad)


def kernel(x, mem, mix_norm_g, xattn_norm_g, mem_norm_g, xattn_wq, xattn_wkv, xattn_wo, ffn_norm_g, ffn_w_gate_up, ffn_w_down, conv_w_in, conv_b_in, conv_dw, conv_dw_b, conv_ln_g, conv_ln_b, conv_w_out, conv_b_out, rwkv_mix, rwkv_w_rkv, rwkv_w0, rwkv_w1, rwkv_w2, rwkv_a0, rwkv_a1, rwkv_a2, rwkv_g1, rwkv_g2, rwkv_k_k, rwkv_k_a, rwkv_r_k, rwkv_gn_g, rwkv_gn_b, rwkv_w_o, sgu_w_in, sgu_b_in, sgu_ln_g, sgu_ln_b, sgu_ws, sgu_bs, sgu_w_out, ret_w_in, ret_w_out, final_norm_g):
    b, t, d = x.shape
    n_mem = mem.shape[1]
    depth = mix_norm_g.shape[0]
    m = b * t
    mem2d = mem.reshape(b * n_mem, d)

    for i in range(depth):
        mixer, j = i % 4, i // 4
        g = mix_norm_g[i]
        if mixer == 0:
            z = _norm_mm(x.reshape(m, d), g, conv_w_in[j].astype(BF16), conv_b_in[j], glu=True,
                         name="conv_in")
            x = _conv_tail(z.reshape(b, t, d), conv_dw[j], conv_dw_b[j], conv_ln_g[j], conv_ln_b[j],
                           conv_w_out[j].astype(BF16), conv_b_out[j], x, name="conv_tail")
        elif mixer == 1:
            lw = _pad_axis(rwkv_w1[j], 1, LANES)
            la = _pad_axis(rwkv_a1[j], 1, LANES)
            lg = _pad_axis(rwkv_g1[j], 1, 2 * LANES)
            p = dict(mix=rwkv_mix[j].astype(F32),
                     wr=rwkv_w_rkv[j, 0].astype(BF16), wk=rwkv_w_rkv[j, 1].astype(BF16),
                     wv=rwkv_w_rkv[j, 2].astype(BF16),
                     w0=_row2(rwkv_w0[j]), w1=lw.astype(BF16),
                     w2=_pad_axis(rwkv_w2[j], 0, LANES).astype(BF16),
                     a0=_row2(rwkv_a0[j]), a1=la.astype(BF16),
                     a2=_pad_axis(rwkv_a2[j], 0, LANES).astype(BF16),
                     g1=lg.astype(BF16), g2=_pad_axis(rwkv_g2[j], 0, 2 * LANES).astype(BF16),
                     k_k=_row2(rwkv_k_k[j]), k_a=_row2(rwkv_k_a[j]), r_k=_row2(rwkv_r_k[j]),
                     gn_g=rwkv_gn_g[j], gn_b=rwkv_gn_b[j], w_o=rwkv_w_o[j].astype(BF16))
            x = _rwkv_mixer(x, g, p)
        elif mixer == 2:
            uv = _norm_mm(x.reshape(m, d), g, sgu_w_in[j].astype(BF16), sgu_b_in[j], act="gelu",
                          name="sgu_in")
            x = _sgu_tail(uv.reshape(b, t, -1), sgu_ln_g[j], sgu_ln_b[j], sgu_ws[j], sgu_bs[j],
                          sgu_w_out[j].astype(BF16), x, name="sgu_tail")
        else:
            qkvg = _norm_mm(x.reshape(m, d), g, ret_w_in[j].astype(BF16), name="ret_in")
            o = _ret_core(qkvg.reshape(b, t, -1), t, d, name="ret_core")
            x = _mm_res(o.reshape(m, -1), ret_w_out[j].astype(BF16), x.reshape(m, d),
                        name="ret_out").reshape(b, t, d)

        kv = _norm_mm(mem2d, mem_norm_g[i], xattn_wkv[i].astype(BF16), name="xattn_kv")
        x = _xattn(x, xattn_norm_g[i], xattn_wq[i].astype(BF16), kv.reshape(b, n_mem, 2 * d),
                   xattn_wo[i].astype(BF16), name="xattn")
        fg = final_norm_g if i == depth - 1 else None
        x = _swiglu(x.reshape(m, d), ffn_norm_g[i], ffn_w_gate_up[i].astype(BF16),
                    ffn_w_down[i].astype(BF16), fg, name="swiglu").reshape(b, t, d)
    return x
```

```python
import functools
import math

import jax
import jax.numpy as jnp
from jax import lax
from jax.experimental import pallas as pl
from jax.experimental.pallas import tpu as pltpu

F32 = jnp.float32
BF16 = jnp.bfloat16

NORM_EPS = 1e-6
LN_EPS = 1e-5

XA_HEADS = 4
CONV_WIDTH = 31
CONV_HALO = 32
RWKV_HEAD_DIM = 64
RWKV_GN_EPS = RWKV_HEAD_DIM * 1e-5
SGU_CHUNK = 128
SGU_GROUPS = 8
RET_HEADS = 4
RET_CHUNK = 512

LANES = 128
SUBLANES = 8
VMEM_LIMIT_BYTES = 56 * 1024 * 1024


def _cparams(*sem):
    return pltpu.CompilerParams(dimension_semantics=sem, vmem_limit_bytes=VMEM_LIMIT_BYTES)


def _rms(x, g):
    return x * lax.rsqrt(jnp.mean(x * x, axis=-1, keepdims=True) + NORM_EPS) * g


def _layernorm(x, g, b, eps):
    mu = jnp.mean(x, axis=-1, keepdims=True)
    xc = x - mu
    var = jnp.mean(xc * xc, axis=-1, keepdims=True)
    return xc * lax.rsqrt(var + eps) * g + b


def _dot(a, b):
    return jnp.dot(a, b, preferred_element_type=F32)


def _dot_nt(a, b):
    return lax.dot_general(a, b, (((1,), (1,)), ((), ())), preferred_element_type=F32)


def _head_sum(x, e):
    return _dot(x.astype(BF16), e)


def _row2(v):
    return v.reshape(1, -1).astype(F32)


def _norm_mm_kernel(*refs, act, glu, has_bias):
    it = iter(refs)
    x_ref, g_ref, w_ref = next(it), next(it), next(it)
    w2_ref = next(it) if glu else None
    b_ref = next(it) if has_bias else None
    b2_ref = next(it) if (glu and has_bias) else None
    o_ref, xn_ref = next(it), next(it)

    @pl.when(pl.program_id(1) == 0)
    def _():
        xn_ref[...] = _rms(x_ref[...], g_ref[...]).astype(BF16)

    xn = xn_ref[...]
    y = _dot(xn, w_ref[...])
    if has_bias:
        y = y + b_ref[...]
    if glu:
        y2 = _dot(xn, w2_ref[...])
        if has_bias:
            y2 = y2 + b2_ref[...]
        y = y * jax.nn.sigmoid(y2)
    elif act == "gelu":
        y = 0.5 * y * (1.0 + lax.erf(y * math.sqrt(0.5)))
    o_ref[...] = y.astype(o_ref.dtype)


def _norm_mm(x2d, g, w, bias=None, *, act=None, glu=False, out_dtype=BF16, tm=1024, tn=1024, name):
    m, k = x2d.shape
    n_w = w.shape[1]
    n_out = n_w // 2 if glu else n_w
    tm, tn = min(tm, m), min(tn, n_out)
    assert m % tm == 0 and n_out % tn == 0
    nb = n_out // tn
    has_bias = bias is not None
    args = [x2d, _row2(g), w]
    specs = [pl.BlockSpec((tm, k), lambda i, j: (i, 0)),
             pl.BlockSpec((1, k), lambda i, j: (0, 0)),
             pl.BlockSpec((k, tn), lambda i, j: (0, j))]
    if glu:
        args.append(w)
        specs.append(pl.BlockSpec((k, tn), lambda i, j: (0, j + nb)))
    if has_bias:
        b2d = _row2(bias)
        args.append(b2d)
        specs.append(pl.BlockSpec((1, tn), lambda i, j: (0, j)))
        if glu:
            args.append(b2d)
            specs.append(pl.BlockSpec((1, tn), lambda i, j: (0, j + nb)))
    return pl.pallas_call(
        functools.partial(_norm_mm_kernel, act=act, glu=glu, has_bias=has_bias),
        out_shape=jax.ShapeDtypeStruct((m, n_out), out_dtype),
        grid=(m // tm, nb),
        in_specs=specs,
        out_specs=pl.BlockSpec((tm, tn), lambda i, j: (i, j)),
        scratch_shapes=[pltpu.VMEM((tm, k), BF16)],
        compiler_params=_cparams("parallel", "arbitrary"),
        name=name,
    )(*args)


def _mm_res_kernel(a_ref, w_ref, r_ref, o_ref):
    o_ref[...] = r_ref[...] + _dot(a_ref[...], w_ref[...])


def _mm_res(a, w, res, *, tm=1024, name):
    m, k = a.shape
    n = w.shape[1]
    assert m % tm == 0
    return pl.pallas_call(
        _mm_res_kernel,
        out_shape=jax.ShapeDtypeStruct((m, n), F32),
        grid=(m // tm,),
        in_specs=[pl.BlockSpec((tm, k), lambda i: (i, 0)),
                  pl.BlockSpec((k, n), lambda i: (0, 0)),
                  pl.BlockSpec((tm, n), lambda i: (i, 0))],
        out_specs=pl.BlockSpec((tm, n), lambda i: (i, 0)),
        compiler_params=_cparams("parallel"),
        name=name,
    )(a, w, res)


def _xattn_kernel(x_ref, g_ref, wq_ref, kv_ref, wo_ref, o_ref, oh_ref, *, heads):
    x = x_ref[0]
    d = x.shape[-1]
    hd = d // heads
    xn = _rms(x, g_ref[...]).astype(BF16)
    q = _dot(xn, wq_ref[...])
    for h in range(heads):
        qh = q[:, h * hd:(h + 1) * hd].astype(BF16)
        kh = kv_ref[0, :, h * hd:(h + 1) * hd]
        vh = kv_ref[0, :, d + h * hd:d + (h + 1) * hd]
        s = _dot_nt(qh, kh) * (hd ** -0.5)
        p = jnp.exp(s - jnp.max(s, axis=-1, keepdims=True))
        p = p / jnp.sum(p, axis=-1, keepdims=True)
        oh_ref[:, h * hd:(h + 1) * hd] = _dot(p.astype(BF16), vh).astype(BF16)
    o_ref[0] = x + _dot(oh_ref[...], wo_ref[...])


def _xattn(x, g, wq, kv, wo, *, tm=1024, name):
    b, t, d = x.shape
    m = kv.shape[1]
    return pl.pallas_call(
        functools.partial(_xattn_kernel, heads=XA_HEADS),
        out_shape=jax.ShapeDtypeStruct((b, t, d), F32),
        grid=(b, t // tm),
        in_specs=[pl.BlockSpec((1, tm, d), lambda bi, i: (bi, i, 0)),
                  pl.BlockSpec((1, d), lambda bi, i: (0, 0)),
                  pl.BlockSpec((d, d), lambda bi, i: (0, 0)),
                  pl.BlockSpec((1, m, 2 * d), lambda bi, i: (bi, 0, 0)),
                  pl.BlockSpec((d, d), lambda bi, i: (0, 0))],
        out_specs=pl.BlockSpec((1, tm, d), lambda bi, i: (bi, i, 0)),
        scratch_shapes=[pltpu.VMEM((tm, d), BF16)],
        compiler_params=_cparams("parallel", "parallel"),
        name=name,
    )(x, _row2(g), wq, kv, wo)


def _swiglu_kernel(*refs, final):
    if final:
        x_ref, g_ref, wg_ref, wu_ref, wd_ref, fg_ref, o_ref, xn_ref, acc_ref = refs
    else:
        x_ref, g_ref, wg_ref, wu_ref, wd_ref, o_ref, xn_ref, acc_ref = refs
    f = pl.program_id(1)

    @pl.when(f == 0)
    def _():
        xn_ref[...] = _rms(x_ref[...], g_ref[...]).astype(BF16)
        acc_ref[...] = jnp.zeros_like(acc_ref)

    xn = xn_ref[...]
    gt = _dot(xn, wg_ref[...])
    up = _dot(xn, wu_ref[...])
    h = (gt * jax.nn.sigmoid(gt) * up).astype(BF16)
    acc_ref[...] += _dot(h, wd_ref[...])

    @pl.when(f == pl.num_programs(1) - 1)
    def _():
        y = x_ref[...] + acc_ref[...]
        if final:
            y = _rms(y, fg_ref[...])
        o_ref[...] = y


def _swiglu(x2d, g, w_gate_up, w_down, final_g=None, *, tm=512, name):
    m, d = x2d.shape
    ff = w_down.shape[0]
    nf = 2
    tf = ff // nf
    assert tf % LANES == 0 and m % tm == 0
    final = final_g is not None
    args = [x2d, _row2(g), w_gate_up, w_gate_up, w_down]
    specs = [pl.BlockSpec((tm, d), lambda i, f: (i, 0)),
             pl.BlockSpec((1, d), lambda i, f: (0, 0)),
             pl.BlockSpec((d, tf), lambda i, f: (0, f)),
             pl.BlockSpec((d, tf), lambda i, f: (0, f + nf)),
             pl.BlockSpec((tf, d), lambda i, f: (f, 0))]
    if final:
        args.append(_row2(final_g))
        specs.append(pl.BlockSpec((1, d), lambda i, f: (0, 0)))
    return pl.pallas_call(
        functools.partial(_swiglu_kernel, final=final),
        out_shape=jax.ShapeDtypeStruct((m, d), F32),
        grid=(m // tm, nf),
        in_specs=specs,
        out_specs=pl.BlockSpec((tm, d), lambda i, f: (i, 0)),
        scratch_shapes=[pltpu.VMEM((tm, d), BF16), pltpu.VMEM((tm, d), F32)],
        compiler_params=_cparams("parallel", "arbitrary"),
        name=name,
    )(*args)


def _conv_tail_kernel(z_ref, zh_ref, dw_ref, dwb_ref, lg_ref, lb_ref, w_ref, b_ref, x_ref,
                      o_ref, zz_ref, sh_ref, c_ref, dwx_ref, *, rows):
    tm, d = z_ref.shape[1], z_ref.shape[2]
    not_first = (pl.program_id(1) != 0).astype(F32)
    zz_ref[0:CONV_HALO, :] = zh_ref[0].astype(F32) * not_first
    zz_ref[CONV_HALO:, :] = z_ref[0].astype(F32)
    for s in range(1, SUBLANES):
        sh_ref[s - 1] = zz_ref[pl.ds(s, sh_ref.shape[1]), :]
    for k in range(CONV_WIDTH):
        dwx_ref[k] = jnp.broadcast_to(dw_ref[k:k + 1, :], (SUBLANES, d))
    dwx_ref[CONV_WIDTH] = jnp.broadcast_to(dwb_ref[...], (SUBLANES, d))
    base = CONV_HALO - (CONV_WIDTH - 1)
    groups = rows // SUBLANES

    def chunk(c, carry):
        r0 = pl.multiple_of(c * rows, rows)
        acc = [dwx_ref[CONV_WIDTH]] * groups
        for k in range(CONV_WIDTH):
            blk, s = divmod(base + k, SUBLANES)
            src = zz_ref if s == 0 else sh_ref.at[s - 1]
            wk = dwx_ref[k]
            for gi in range(groups):
                acc[gi] = acc[gi] + src[pl.ds(r0 + (blk + gi) * SUBLANES, SUBLANES), :] * wk
        for gi in range(groups):
            c_ref[pl.ds(r0 + gi * SUBLANES, SUBLANES), :] = acc[gi]
        return carry

    lax.fori_loop(0, tm // rows, chunk, 0)
    y = _layernorm(c_ref[...], lg_ref[...], lb_ref[...], LN_EPS)
    act = (y * jax.nn.sigmoid(y)).astype(BF16)
    o_ref[0] = x_ref[0] + _dot(act, w_ref[...]) + b_ref[...]


def _conv_tail(z, dw, dw_b, ln_g, ln_b, w_out, b_out, x, *, tm=512, rows=32, name):
    b, t, d = x.shape
    hb = tm // CONV_HALO
    vec = pl.BlockSpec((1, d), lambda bi, i: (0, 0))
    return pl.pallas_call(
        functools.partial(_conv_tail_kernel, rows=rows),
        out_shape=jax.ShapeDtypeStruct((b, t, d), F32),
        grid=(b, t // tm),
        in_specs=[pl.BlockSpec((1, tm, d), lambda bi, i: (bi, i, 0)),
                  pl.BlockSpec((1, CONV_HALO, d), lambda bi, i: (bi, jnp.maximum(i * hb - 1, 0), 0)),
                  pl.BlockSpec((CONV_WIDTH, d), lambda bi, i: (0, 0)),
                  vec, vec, vec,
                  pl.BlockSpec((d, d), lambda bi, i: (0, 0)),
                  vec,
                  pl.BlockSpec((1, tm, d), lambda bi, i: (bi, i, 0))],
        out_specs=pl.BlockSpec((1, tm, d), lambda bi, i: (bi, i, 0)),
        scratch_shapes=[pltpu.VMEM((tm + CONV_HALO, d), F32),
                        pltpu.VMEM((SUBLANES - 1, tm + CONV_HALO - SUBLANES, d), F32),
                        pltpu.VMEM((tm, d), F32),
                        pltpu.VMEM((CONV_WIDTH + 1, SUBLANES, d), F32)],
        compiler_params=_cparams("parallel", "parallel"),
        name=name,
    )(z, z, dw.astype(F32), _row2(dw_b), _row2(ln_g), _row2(ln_b), w_out, _row2(b_out), x)


def _softplus(x):
    return jnp.maximum(x, 0.0) + jnp.log(1.0 + jnp.exp(-jnp.abs(x)))


def _rwkv_prep_kernel(x_ref, xh_ref, g_ref, mix_ref, wr_ref, wk_ref, wv_ref, w0_ref, w1_ref, w2_ref,
                      a0_ref, a1_ref, a2_ref, g1_ref, g2_ref, kk_ref, ka_ref, rk_ref, e_ref, e2_ref, et_ref,
                      e_o, an_o, b_o, k_o, wr_o, v_o, bv_o, g_o, s_o, hs_ref):
    tm = x_ref.shape[1]
    gn = g_ref[...]
    hn = _rms(x_ref[0], gn)
    not_first = (pl.program_id(1) != 0).astype(F32)
    hs_ref[0:SUBLANES, :] = _rms(xh_ref[0], gn) * not_first
    hs_ref[SUBLANES:, :] = hn
    delta = hs_ref[pl.ds(SUBLANES - 1, tm), :] - hn

    def mixed(c):
        return (hn + delta * mix_ref[c:c + 1, :]).astype(BF16)

    r = _dot(mixed(0), wr_ref[...])
    k = _dot(mixed(2), wk_ref[...])
    v = _dot(mixed(3), wv_ref[...])
    w_lora = _dot(jnp.tanh(_dot(mixed(1), w1_ref[...])).astype(BF16), w2_ref[...])
    w = -_softplus(-(w0_ref[...] + w_lora)) - 0.5
    e_neg_log = jnp.exp(w).astype(BF16)
    decay = jnp.exp(-e_neg_log.astype(F32))
    a = jax.nn.sigmoid(a0_ref[...] + _dot(_dot(mixed(4), a1_ref[...]).astype(BF16), a2_ref[...]))
    gate = _dot(jax.nn.sigmoid(_dot(mixed(5), g1_ref[...])).astype(BF16), g2_ref[...])

    e, e2, et = e_ref[...], e2_ref[...], et_ref[...]
    kk = k * kk_ref[...]
    ss = _dot(_head_sum(kk * kk, e).astype(BF16), et)
    kk = kk / jnp.maximum(jnp.sqrt(ss), 1e-12)
    k = k * (1.0 + (a - 1.0) * ka_ref[...])
    b = kk * a
    bonus = _dot(_head_sum(r * k * rk_ref[...], e).astype(BF16), et)

    e_o[0] = e_neg_log
    an_o[0] = (-kk).astype(an_o.dtype)
    b_o[0] = b.astype(b_o.dtype)
    k_o[0] = k.astype(k_o.dtype)
    wr_o[0] = (decay * r).astype(wr_o.dtype)
    v_o[0] = v.astype(v_o.dtype)
    bv_o[0] = (bonus * v).astype(bv_o.dtype)
    g_o[0] = gate.astype(g_o.dtype)
    s_o[0] = _head_sum(b * r, e) + _head_sum(k * r, e2)


def _rwkv_prep(x, g, p, e, e2, et, *, tm=256, name):
    b, t, d = x.shape
    hb = tm // SUBLANES
    tile = pl.BlockSpec((1, tm, d), lambda bi, i: (bi, i, 0))
    vec = pl.BlockSpec((1, d), lambda bi, i: (0, 0))

    def full(a):
        return pl.BlockSpec(a.shape, lambda bi, i: (0,) * a.ndim)

    consts = [p["mix"], p["wr"], p["wk"], p["wv"], p["w0"], p["w1"], p["w2"], p["a0"], p["a1"], p["a2"],
              p["g1"], p["g2"], p["k_k"], p["k_a"], p["r_k"], e, e2, et]
    bf16_tile = jax.ShapeDtypeStruct((b, t, d), BF16)
    return pl.pallas_call(
        _rwkv_prep_kernel,
        out_shape=[bf16_tile] * 8 + [jax.ShapeDtypeStruct((b, t, LANES), F32)],
        grid=(b, t // tm),
        in_specs=[tile,
                  pl.BlockSpec((1, SUBLANES, d), lambda bi, i: (bi, jnp.maximum(i * hb - 1, 0), 0)),
                  vec] + [full(a) for a in consts],
        out_specs=[tile] * 8 + [pl.BlockSpec((1, tm, LANES), lambda bi, i: (bi, i, 0))],
        scratch_shapes=[pltpu.VMEM((tm + SUBLANES, d), F32)],
        compiler_params=_cparams("parallel", "parallel"),
        name=name,
    )(x, x, _row2(g), *consts)


def _rwkv_scan_kernel(e_ref, b_ref, k_ref, v_ref, sc_ref, an_ref, wr_ref, ann_ref, wrn_ref,
                      y_ref, s_ref, sa_ref, ys_ref, bs_ref, ks_ref, as_ref, ws_ref, pe_ref):
    tb, nk, nh = e_ref.shape
    nv = s_ref.shape[1]

    @pl.when(pl.program_id(0) == 0)
    def _():
        s_ref[...] = jnp.zeros_like(s_ref)
        sa_ref[...] = jnp.zeros_like(sa_ref)
        ys_ref[...] = jnp.zeros_like(ys_ref)

    cum = jnp.zeros((nk, nh), F32)
    for t in range(tb):
        cum = cum + e_ref[t].astype(F32)
        grow, shrink = jnp.exp(cum), jnp.exp(-cum)
        bs_ref[t] = b_ref[t].astype(F32) * grow
        ks_ref[t] = k_ref[t].astype(F32) * grow
        a_next = an_ref[t + 1] if t + 1 < tb else ann_ref[0]
        w_next = wr_ref[t + 1] if t + 1 < tb else wrn_ref[0]
        as_ref[t] = a_next.astype(F32) * shrink
        ws_ref[t] = w_next.astype(F32) * shrink
    pe_ref[...] = jnp.exp(-cum)

    def step(t, ln, last):
        sa, ys, v = sa_ref[:, ln], ys_ref[:, ln], v_ref[t, :, ln].astype(F32)
        y = ys + sa * sc_ref[t, 0:1, ln] + v * sc_ref[t, 1:2, ln]
        yc = y - jnp.sum(y, axis=0, keepdims=True) * (1.0 / nv)
        var = jnp.sum(yc * yc, axis=0, keepdims=True) * (1.0 / nv)
        y_ref[t, :, ln] = (yc * lax.rsqrt(var + RWKV_GN_EPS)).astype(y_ref.dtype)
        sa_n = jnp.zeros_like(sa)
        ys_n = jnp.zeros_like(sa)
        for j in range(nk):
            s = s_ref[j, :, ln] + sa * bs_ref[t, j:j + 1, ln] + v * ks_ref[t, j:j + 1, ln]
            s_ref[j, :, ln] = s * pe_ref[j:j + 1, ln] if last else s
            sa_n = sa_n + s * as_ref[t, j:j + 1, ln]
            ys_n = ys_n + s * ws_ref[t, j:j + 1, ln]
        sa_ref[:, ln] = sa_n
        ys_ref[:, ln] = ys_n

    for hb in range(nh // LANES):
        ln = slice(hb * LANES, (hb + 1) * LANES)

        def body(t, carry, ln=ln):
            step(t, ln, False)
            return carry

        lax.fori_loop(0, tb - 1, body, 0)
        step(tb - 1, ln, True)


def _rwkv_scan(e, an, bb, kk, wr, v, sc, *, tb=16, name):
    t, n, nh = e.shape
    blk = pl.BlockSpec((tb, n, nh), lambda i: (i, 0, 0))
    row = pl.BlockSpec((tb, 2, nh), lambda i: (i, 0, 0))
    nxt = pl.BlockSpec((1, n, nh), lambda i: (jnp.minimum((i + 1) * tb, t - 1), 0, 0))
    step_rows = pltpu.VMEM((tb, n, nh), F32)
    return pl.pallas_call(
        _rwkv_scan_kernel,
        out_shape=jax.ShapeDtypeStruct((t, n, nh), BF16),
        grid=(t // tb,),
        in_specs=[blk, blk, blk, blk, row, blk, blk, nxt, nxt],
        out_specs=blk,
        scratch_shapes=[pltpu.VMEM((n, n, nh), F32), pltpu.VMEM((n, nh), F32), pltpu.VMEM((n, nh), F32),
                        step_rows, step_rows, step_rows, step_rows, pltpu.VMEM((n, nh), F32)],
        compiler_params=_cparams("arbitrary"),
        name=name,
    )(e, bb, kk, v, sc, an, wr, an, wr)


def _rwkv_post_kernel(y_ref, bv_ref, g_ref, x_ref, gg_ref, gb_ref, w_ref, o_ref):
    z = ((y_ref[...].astype(F32) * gg_ref[...] + gb_ref[...] + bv_ref[...].astype(F32))
         * g_ref[...].astype(F32))
    o_ref[...] = x_ref[...] + _dot(z.astype(BF16), w_ref[...])


def _rwkv_post(y, bv, g, x2d, gn_g, gn_b, w_o, *, tm=1024, name):
    m, d = x2d.shape
    tile = pl.BlockSpec((tm, d), lambda i: (i, 0))
    vec = pl.BlockSpec((1, d), lambda i: (0, 0))
    return pl.pallas_call(
        _rwkv_post_kernel,
        out_shape=jax.ShapeDtypeStruct((m, d), F32),
        grid=(m // tm,),
        in_specs=[tile, tile, tile, tile, vec, vec, pl.BlockSpec((d, d), lambda i: (0, 0))],
        out_specs=tile,
        compiler_params=_cparams("parallel"),
        name=name,
    )(y, bv, g, x2d, _row2(gn_g), _row2(gn_b), w_o)


def _rwkv_mixer(x, g, p):
    b, t, d = x.shape
    n = RWKV_HEAD_DIM
    h = d // n
    lane = jnp.arange(LANES)[None, :]
    head = (jnp.arange(d) // n)[:, None]
    e = (head == lane).astype(BF16)
    e2 = (head + h == lane).astype(BF16)
    et = e.T
    el, an, bb, kk, wr, v, bv, gate, s = _rwkv_prep(x, g, p, e, e2, et, name="rwkv_prep")

    def tm(a):
        return a.reshape(b, t, h, n).transpose(1, 3, 0, 2).reshape(t, n, b * h)

    sc = s[:, :, :2 * h].reshape(b, t, 2, h).transpose(1, 2, 0, 3).reshape(t, 2, b * h)
    y = _rwkv_scan(tm(el), tm(an), tm(bb), tm(kk), tm(wr), tm(v), sc, name="rwkv_scan")
    y = y.reshape(t, n, b, h).transpose(2, 0, 3, 1).reshape(b * t, d)
    out = _rwkv_post(y, bv.reshape(b * t, d), gate.reshape(b * t, d), x.reshape(b * t, d),
                     p["gn_g"], p["gn_b"], p["w_o"], name="rwkv_post")
    return out.reshape(b, t, d)


def _sgu_tail_kernel(u_ref, v_ref, lg_ref, lb_ref, ws_ref, bst_ref, w_ref, x_ref, o_ref, vn_ref, gt_ref):
    tm, width = u_ref.shape[1], u_ref.shape[2]
    c = SGU_CHUNK
    gd = width // SGU_GROUPS
    vn_ref[...] = _layernorm(v_ref[0].astype(F32), lg_ref[...], lb_ref[...], LN_EPS).astype(BF16)
    tri = lax.broadcasted_iota(jnp.int32, (c, c), 0) >= lax.broadcasted_iota(jnp.int32, (c, c), 1)
    for gi in range(SGU_GROUPS):
        wsm = jnp.where(tri, ws_ref[gi], 0.0).astype(BF16)
        bias = bst_ref[:, gi:gi + 1]
        cols = slice(gi * gd, (gi + 1) * gd)
        for ci in range(tm // c):
            rows = slice(ci * c, (ci + 1) * c)
            mixed = _dot(wsm, vn_ref[rows, cols]) + bias
            gt_ref[rows, cols] = (u_ref[0, rows, cols].astype(F32) * mixed).astype(BF16)
    o_ref[0] = x_ref[0] + _dot(gt_ref[...], w_ref[...])


def _sgu_tail(uv, ln_g, ln_b, ws, bs, w_out, x, *, tm=512, name):
    b, t, d = x.shape
    width = uv.shape[-1] // 2
    return pl.pallas_call(
        _sgu_tail_kernel,
        out_shape=jax.ShapeDtypeStruct((b, t, d), F32),
        grid=(b, t // tm),
        in_specs=[pl.BlockSpec((1, tm, width), lambda bi, i: (bi, i, 0)),
                  pl.BlockSpec((1, tm, width), lambda bi, i: (bi, i, 1)),
                  pl.BlockSpec((1, width), lambda bi, i: (0, 0)),
                  pl.BlockSpec((1, width), lambda bi, i: (0, 0)),
                  pl.BlockSpec(ws.shape, lambda bi, i: (0, 0, 0)),
                  pl.BlockSpec((SGU_CHUNK, SGU_GROUPS), lambda bi, i: (0, 0)),
                  pl.BlockSpec((width, d), lambda bi, i: (0, 0)),
                  pl.BlockSpec((1, tm, d), lambda bi, i: (bi, i, 0))],
        out_specs=pl.BlockSpec((1, tm, d), lambda bi, i: (bi, i, 0)),
        scratch_shapes=[pltpu.VMEM((tm, width), BF16), pltpu.VMEM((tm, width), BF16)],
        compiler_params=_cparams("parallel", "parallel"),
        name=name,
    )(uv, uv, _row2(ln_g), _row2(ln_b), ws.astype(F32), bs.T.astype(F32), w_out, x)


def _ret_core_kernel(q_ref, k_ref, v_ref, g_ref, sin_ref, cos_ref, rot_ref, dm_ref, qd_ref, kd_ref, cd_ref,
                     o_ref, r_ref):
    @pl.when(pl.program_id(2) == 0)
    def _():
        r_ref[...] = jnp.zeros_like(r_ref)

    sin, cos, rot = sin_ref[...], cos_ref[...], rot_ref[...]
    qb, kb, vb = q_ref[0], k_ref[0], v_ref[0]
    q = qb.astype(F32) * cos + _dot(qb, rot) * sin
    k = (kb.astype(F32) * cos + _dot(kb, rot) * sin) * (qb.shape[-1] ** -0.5)
    scores = _dot_nt(q.astype(BF16), k.astype(BF16)) * dm_ref[0]
    inner = _dot(scores.astype(BF16), vb)
    r = r_ref[...]
    cross = _dot((q * qd_ref[0]).astype(BF16), r.astype(BF16))
    ks = (k * kd_ref[0]).astype(BF16)
    kv = lax.dot_general(ks, vb, (((0,), (0,)), ((), ())), preferred_element_type=F32)
    r_ref[...] = r * cd_ref[0, 0:1, 0:1] + kv
    o = inner + cross
    o = o * lax.rsqrt(jnp.mean(o * o, axis=-1, keepdims=True) + NORM_EPS)
    gate = g_ref[0].astype(F32)
    o_ref[0] = (gate * jax.nn.sigmoid(gate) * o).astype(o_ref.dtype)


def _ret_core(qkvg, t, d, *, name):
    b = qkvg.shape[0]
    hh, c = RET_HEADS, RET_CHUNK
    dk = d // hh
    dv = 2 * dk
    angle = jnp.repeat(1.0 / (10000.0 ** jnp.linspace(0.0, 1.0, dk // 2, dtype=F32)), 2)
    theta = jnp.arange(t, dtype=F32)[:, None] * angle[None]
    log_gamma = jnp.log(1.0 - 2.0 ** (-5.0 - jnp.arange(hh, dtype=F32)))
    idx = jnp.arange(c, dtype=F32)
    diff = idx[:, None] - idx[None, :]
    dm = jnp.where(diff[None] >= 0, jnp.exp(log_gamma[:, None, None] * jnp.maximum(diff, 0.0)[None]), 0.0)
    qd = jnp.exp(log_gamma[:, None] * (idx + 1.0))[:, :, None]
    kd = jnp.exp(log_gamma[:, None] * (c - 1.0 - idx))[:, :, None]
    cd = jnp.broadcast_to(jnp.exp(log_gamma * c)[:, None, None], (hh, SUBLANES, LANES))
    src, dst = jnp.arange(dk)[:, None], jnp.arange(dk)[None, :]
    rot = (jnp.where((dst % 2 == 1) & (src == dst - 1), 1.0, 0.0)
           - jnp.where((dst % 2 == 0) & (src == dst + 1), 1.0, 0.0)).astype(BF16)
    nq = d // dk
    nv = 2 * d // dv
    return pl.pallas_call(
        _ret_core_kernel,
        out_shape=jax.ShapeDtypeStruct((b, t, hh * dv), BF16),
        grid=(b, hh, t // c),
        in_specs=[pl.BlockSpec((1, c, dk), lambda bi, h, n: (bi, n, h)),
                  pl.BlockSpec((1, c, dk), lambda bi, h, n: (bi, n, nq + h)),
                  pl.BlockSpec((1, c, dv), lambda bi, h, n: (bi, n, nv + h)),
                  pl.BlockSpec((1, c, dv), lambda bi, h, n: (bi, n, nv + hh + h)),
                  pl.BlockSpec((c, dk), lambda bi, h, n: (n, 0)),
                  pl.BlockSpec((c, dk), lambda bi, h, n: (n, 0)),
                  pl.BlockSpec((dk, dk), lambda bi, h, n: (0, 0)),
                  pl.BlockSpec((1, c, c), lambda bi, h, n: (h, 0, 0)),
                  pl.BlockSpec((1, c, 1), lambda bi, h, n: (h, 0, 0)),
                  pl.BlockSpec((1, c, 1), lambda bi, h, n: (h, 0, 0)),
                  pl.BlockSpec((1, SUBLANES, LANES), lambda bi, h, n: (h, 0, 0))],
        out_specs=pl.BlockSpec((1, c, dv), lambda bi, h, n: (bi, n, h)),
        scratch_shapes=[pltpu.VMEM((dk, dv), F32)],
        compiler_params=_cparams("parallel", "parallel", "arbitrary"),
        name=name,
    )(qkvg, qkvg, qkvg, qkvg, jnp.sin(theta), jnp.cos(theta), rot, dm, qd, kd, cd)


def _pad_axis(a, axis, to):
    pad = [(0, 0)] * a.ndim
    pad[axis] = (0, to - a.shape[axis])
    return jnp.pad(a, pad)


def kernel(x, mem, mix_norm_g, xattn_norm_g, mem_norm_g, xattn_wq, xattn_wkv, xattn_wo, ffn_norm_g, ffn_w_gate_up, ffn_w_down, conv_w_in, conv_b_in, conv_dw, conv_dw_b, conv_ln_g, conv_ln_b, conv_w_out, conv_b_out, rwkv_mix, rwkv_w_rkv, rwkv_w0, rwkv_w1, rwkv_w2, rwkv_a0, rwkv_a1, rwkv_a2, rwkv_g1, rwkv_g2, rwkv_k_k, rwkv_k_a, rwkv_r_k, rwkv_gn_g, rwkv_gn_b, rwkv_w_o, sgu_w_in, sgu_b_in, sgu_ln_g, sgu_ln_b, sgu_ws, sgu_bs, sgu_w_out, ret_w_in, ret_w_out, final_norm_g):
    b, t, d = x.shape
    n_mem = mem.shape[1]
    depth = mix_norm_g.shape[0]
    m = b * t
    mem2d = mem.reshape(b * n_mem, d)

    for i in range(depth):
        mixer, j = i % 4, i // 4
        g = mix_norm_g[i]
        if mixer == 0:
            z = _norm_mm(x.reshape(m, d), g, conv_w_in[j].astype(BF16), conv_b_in[j], glu=True,
                         name="conv_in")
            x = _conv_tail(z.reshape(b, t, d), conv_dw[j], conv_dw_b[j], conv_ln_g[j], conv_ln_b[j],
                           conv_w_out[j].astype(BF16), conv_b_out[j], x, name="conv_tail")
        elif mixer == 1:
            lw = _pad_axis(rwkv_w1[j], 1, LANES)
            la = _pad_axis(rwkv_a1[j], 1, LANES)
            lg = _pad_axis(rwkv_g1[j], 1, 2 * LANES)
            p = dict(mix=rwkv_mix[j].astype(F32),
                     wr=rwkv_w_rkv[j, 0].astype(BF16), wk=rwkv_w_rkv[j, 1].astype(BF16),
                     wv=rwkv_w_rkv[j, 2].astype(BF16),
                     w0=_row2(rwkv_w0[j]), w1=lw.astype(BF16),
                     w2=_pad_axis(rwkv_w2[j], 0, LANES).astype(BF16),
                     a0=_row2(rwkv_a0[j]), a1=la.astype(BF16),
                     a2=_pad_axis(rwkv_a2[j], 0, LANES).astype(BF16),
                     g1=lg.astype(BF16), g2=_pad_axis(rwkv_g2[j], 0, 2 * LANES).astype(BF16),
                     k_k=_row2(rwkv_k_k[j]), k_a=_row2(rwkv_k_a[j]), r_k=_row2(rwkv_r_k[j]),
                     gn_g=rwkv_gn_g[j], gn_b=rwkv_gn_b[j], w_o=rwkv_w_o[j].astype(BF16))
            x = _rwkv_mixer(x, g, p)
        elif mixer == 2:
            uv = _norm_mm(x.reshape(m, d), g, sgu_w_in[j].astype(BF16), sgu_b_in[j], act="gelu",
                          name="sgu_in")
            x = _sgu_tail(uv.reshape(b, t, -1), sgu_ln_g[j], sgu_ln_b[j], sgu_ws[j], sgu_bs[j],
                          sgu_w_out[j].astype(BF16), x, name="sgu_tail")
        else:
            qkvg = _norm_mm(x.reshape(m, d), g, ret_w_in[j].astype(BF16), name="ret_in")
            o = _ret_core(qkvg.reshape(b, t, -1), t, d, name="ret_core")
            x = _mm_res(o.reshape(m, -1), ret_w_out[j].astype(BF16), x.reshape(m, d),
                        name="ret_out").reshape(b, t, d)

        kv = _norm_mm(mem2d, mem_norm_g[i], xattn_wkv[i].astype(BF16), name="xattn_kv")
        x = _xattn(x, xattn_norm_g[i], xattn_wq[i].astype(BF16), kv.reshape(b, n_mem, 2 * d),
                   xattn_wo[i].astype(BF16), name="xattn")
        fg = final_norm_g if i == depth - 1 else None
        x = _swiglu(x.reshape(m, d), ffn_norm_g[i], ffn_w_gate_up[i].astype(BF16),
                    ffn_w_down[i].astype(BF16), fg, name="swiglu").reshape(b, t, d)
    return x
```

```python
import functools
import math

import jax
import jax.numpy as jnp
from jax import lax
from jax.experimental import pallas as pl
from jax.experimental.pallas import tpu as pltpu

F32 = jnp.float32
BF16 = jnp.bfloat16

NORM_EPS = 1e-6
LN_EPS = 1e-5

XA_HEADS = 4
CONV_WIDTH = 31
CONV_HALO = 32
RWKV_HEAD_DIM = 64
RWKV_GN_EPS = RWKV_HEAD_DIM * 1e-5
SGU_CHUNK = 128
SGU_GROUPS = 8
RET_HEADS = 4
RET_CHUNK = 512

LANES = 128
MXU_WIDTH = 256
SUBLANES = 8
VMEM_LIMIT_BYTES = 56 * 1024 * 1024


def _cparams(*sem):
    return pltpu.CompilerParams(dimension_semantics=sem, vmem_limit_bytes=VMEM_LIMIT_BYTES)


def _rms(x, g):
    return x * lax.rsqrt(jnp.mean(x * x, axis=-1, keepdims=True) + NORM_EPS) * g


def _layernorm(x, g, b, eps):
    mu = jnp.mean(x, axis=-1, keepdims=True)
    xc = x - mu
    var = jnp.mean(xc * xc, axis=-1, keepdims=True)
    return xc * lax.rsqrt(var + eps) * g + b


def _dot(a, b):
    return jnp.dot(a, b, preferred_element_type=F32)


def _dot_nt(a, b):
    return lax.dot_general(a, b, (((1,), (1,)), ((), ())), preferred_element_type=F32)


def _head_sum(x, e):
    return _dot(x.astype(BF16), e)


def _row2(v):
    return v.reshape(1, -1).astype(F32)


def _norm_mm_kernel(*refs, act, glu, has_bias, chunks):
    if has_bias:
        x_ref, g_ref, w_ref, b_ref, o_ref = refs
    else:
        x_ref, g_ref, w_ref, o_ref = refs
    xn = _rms(x_ref[...], g_ref[...]).astype(BF16)
    n_out = o_ref.shape[1]
    for lo, hi in chunks:
        y = _dot(xn, w_ref[:, lo:hi])
        if has_bias:
            y = y + b_ref[:, lo:hi]
        if glu:
            y2 = _dot(xn, w_ref[:, n_out + lo:n_out + hi])
            if has_bias:
                y2 = y2 + b_ref[:, n_out + lo:n_out + hi]
            y = y * jax.nn.sigmoid(y2)
        elif act == "gelu":
            y = 0.5 * y * (1.0 + lax.erf(y * math.sqrt(0.5)))
        o_ref[:, lo:hi] = y.astype(o_ref.dtype)


def _norm_mm(x2d, g, w, bias=None, *, act=None, glu=False, out_dtype=BF16, tm=512, chunk=4 * MXU_WIDTH, name):
    m, k = x2d.shape
    n_w = w.shape[1]
    n_out = n_w // 2 if glu else n_w
    assert m % tm == 0 and n_out % MXU_WIDTH == 0
    chunks = tuple((lo, min(lo + chunk, n_out)) for lo in range(0, n_out, chunk))
    has_bias = bias is not None
    resident = dict(pipeline_mode=pl.Buffered(1))
    args = [x2d, _row2(g), w]
    specs = [pl.BlockSpec((tm, k), lambda i: (i, 0)),
             pl.BlockSpec((1, k), lambda i: (0, 0)),
             pl.BlockSpec((k, n_w), lambda i: (0, 0), **resident)]
    if has_bias:
        args.append(_row2(bias))
        specs.append(pl.BlockSpec((1, n_w), lambda i: (0, 0)))
    return pl.pallas_call(
        functools.partial(_norm_mm_kernel, act=act, glu=glu, has_bias=has_bias, chunks=chunks),
        out_shape=jax.ShapeDtypeStruct((m, n_out), out_dtype),
        grid=(m // tm,),
        in_specs=specs,
        out_specs=pl.BlockSpec((tm, n_out), lambda i: (i, 0)),
        compiler_params=_cparams("parallel"),
        name=name,
    )(*args)


def _mm_res_kernel(a_ref, w_ref, r_ref, o_ref):
    o_ref[...] = r_ref[...] + _dot(a_ref[...], w_ref[...])


def _mm_res(a, w, res, *, tm=1024, name):
    m, k = a.shape
    n = w.shape[1]
    assert m % tm == 0
    return pl.pallas_call(
        _mm_res_kernel,
        out_shape=jax.ShapeDtypeStruct((m, n), F32),
        grid=(m // tm,),
        in_specs=[pl.BlockSpec((tm, k), lambda i: (i, 0)),
                  pl.BlockSpec((k, n), lambda i: (0, 0)),
                  pl.BlockSpec((tm, n), lambda i: (i, 0))],
        out_specs=pl.BlockSpec((tm, n), lambda i: (i, 0)),
        compiler_params=_cparams("parallel"),
        name=name,
    )(a, w, res)


def _xattn_kernel(x_ref, g_ref, wq_ref, kv_ref, wo_ref, o_ref, oh_ref, *, heads):
    x = x_ref[0]
    d = x.shape[-1]
    hd = d // heads
    xn = _rms(x, g_ref[...]).astype(BF16)
    q = _dot(xn, wq_ref[...])
    for h in range(heads):
        qh = q[:, h * hd:(h + 1) * hd].astype(BF16)
        kh = kv_ref[0, :, h * hd:(h + 1) * hd]
        vh = kv_ref[0, :, d + h * hd:d + (h + 1) * hd]
        s = _dot_nt(qh, kh) * (hd ** -0.5)
        p = jnp.exp(s - jnp.max(s, axis=-1, keepdims=True))
        p = p / jnp.sum(p, axis=-1, keepdims=True)
        oh_ref[:, h * hd:(h + 1) * hd] = _dot(p.astype(BF16), vh).astype(BF16)
    o_ref[0] = x + _dot(oh_ref[...], wo_ref[...])


def _xattn(x, g, wq, kv, wo, *, tm=1024, name):
    b, t, d = x.shape
    m = kv.shape[1]
    assert t % tm == 0
    return pl.pallas_call(
        functools.partial(_xattn_kernel, heads=XA_HEADS),
        out_shape=jax.ShapeDtypeStruct((b, t, d), F32),
        grid=(b, t // tm),
        in_specs=[pl.BlockSpec((1, tm, d), lambda bi, i: (bi, i, 0)),
                  pl.BlockSpec((1, d), lambda bi, i: (0, 0)),
                  pl.BlockSpec((d, d), lambda bi, i: (0, 0)),
                  pl.BlockSpec((1, m, 2 * d), lambda bi, i: (bi, 0, 0)),
                  pl.BlockSpec((d, d), lambda bi, i: (0, 0))],
        out_specs=pl.BlockSpec((1, tm, d), lambda bi, i: (bi, i, 0)),
        scratch_shapes=[pltpu.VMEM((tm, d), BF16)],
        compiler_params=_cparams("parallel", "parallel"),
        name=name,
    )(x, _row2(g), wq, kv, wo)


def _swiglu_kernel(*refs, final, chunks):
    if final:
        x_ref, g_ref, wg_ref, wu_ref, wd_ref, fg_ref, o_ref = refs
    else:
        x_ref, g_ref, wg_ref, wu_ref, wd_ref, o_ref = refs
    x = x_ref[...]
    xn = _rms(x, g_ref[...]).astype(BF16)
    y = x
    for lo, hi in chunks:
        gt = _dot(xn, wg_ref[:, lo:hi])
        up = _dot(xn, wu_ref[:, lo:hi])
        h = (gt * jax.nn.sigmoid(gt) * up).astype(BF16)
        y = y + _dot(h, wd_ref[lo:hi, :])
    if final:
        y = _rms(y, fg_ref[...])
    o_ref[...] = y


def _swiglu(x2d, g, w_gate_up, w_down, final_g=None, *, tm=512, chunk=3 * MXU_WIDTH, name):
    m, d = x2d.shape
    ff = w_down.shape[0]
    assert ff % MXU_WIDTH == 0 and m % tm == 0
    chunks = tuple((lo, min(lo + chunk, ff)) for lo in range(0, ff, chunk))
    final = final_g is not None
    resident = dict(pipeline_mode=pl.Buffered(1))
    args = [x2d, _row2(g), w_gate_up, w_gate_up, w_down]
    specs = [pl.BlockSpec((tm, d), lambda i: (i, 0)),
             pl.BlockSpec((1, d), lambda i: (0, 0)),
             pl.BlockSpec((d, ff), lambda i: (0, 0), **resident),
             pl.BlockSpec((d, ff), lambda i: (0, 1), **resident),
             pl.BlockSpec((ff, d), lambda i: (0, 0), **resident)]
    if final:
        args.append(_row2(final_g))
        specs.append(pl.BlockSpec((1, d), lambda i: (0, 0)))
    return pl.pallas_call(
        functools.partial(_swiglu_kernel, final=final, chunks=chunks),
        out_shape=jax.ShapeDtypeStruct((m, d), F32),
        grid=(m // tm,),
        in_specs=specs,
        out_specs=pl.BlockSpec((tm, d), lambda i: (i, 0)),
        compiler_params=_cparams("parallel"),
        name=name,
    )(*args)


def _conv_tail_kernel(z_ref, zh_ref, dw_ref, dwb_ref, lg_ref, lb_ref, w_ref, b_ref, x_ref,
                      o_ref, zz_ref, sh_ref, c_ref, dwx_ref, *, rows):
    tm, d = z_ref.shape[1], z_ref.shape[2]
    not_first = (pl.program_id(1) != 0).astype(F32)
    zz_ref[0:CONV_HALO, :] = zh_ref[0].astype(F32) * not_first
    zz_ref[CONV_HALO:, :] = z_ref[0].astype(F32)
    for s in range(1, SUBLANES):
        sh_ref[s - 1] = zz_ref[pl.ds(s, sh_ref.shape[1]), :]
    for k in range(CONV_WIDTH):
        dwx_ref[k] = jnp.broadcast_to(dw_ref[k:k + 1, :], (SUBLANES, d))
    dwx_ref[CONV_WIDTH] = jnp.broadcast_to(dwb_ref[...], (SUBLANES, d))
    base = CONV_HALO - (CONV_WIDTH - 1)
    groups = rows // SUBLANES

    def chunk(c, carry):
        r0 = pl.multiple_of(c * rows, rows)
        acc = [dwx_ref[CONV_WIDTH]] * groups
        for k in range(CONV_WIDTH):
            blk, s = divmod(base + k, SUBLANES)
            src = zz_ref if s == 0 else sh_ref.at[s - 1]
            wk = dwx_ref[k]
            for gi in range(groups):
                acc[gi] = acc[gi] + src[pl.ds(r0 + (blk + gi) * SUBLANES, SUBLANES), :] * wk
        for gi in range(groups):
            c_ref[pl.ds(r0 + gi * SUBLANES, SUBLANES), :] = acc[gi]
        return carry

    lax.fori_loop(0, tm // rows, chunk, 0)
    y = _layernorm(c_ref[...], lg_ref[...], lb_ref[...], LN_EPS)
    act = (y * jax.nn.sigmoid(y)).astype(BF16)
    o_ref[0] = x_ref[0] + _dot(act, w_ref[...]) + b_ref[...]


def _conv_tail(z, dw, dw_b, ln_g, ln_b, w_out, b_out, x, *, tm=512, rows=32, name):
    b, t, d = x.shape
    assert t % tm == 0 and tm % rows == 0 and rows % SUBLANES == 0
    hb = tm // CONV_HALO
    vec = pl.BlockSpec((1, d), lambda bi, i: (0, 0))
    return pl.pallas_call(
        functools.partial(_conv_tail_kernel, rows=rows),
        out_shape=jax.ShapeDtypeStruct((b, t, d), F32),
        grid=(b, t // tm),
        in_specs=[pl.BlockSpec((1, tm, d), lambda bi, i: (bi, i, 0)),
                  pl.BlockSpec((1, CONV_HALO, d), lambda bi, i: (bi, jnp.maximum(i * hb - 1, 0), 0)),
                  pl.BlockSpec((CONV_WIDTH, d), lambda bi, i: (0, 0)),
                  vec, vec, vec,
                  pl.BlockSpec((d, d), lambda bi, i: (0, 0)),
                  vec,
                  pl.BlockSpec((1, tm, d), lambda bi, i: (bi, i, 0))],
        out_specs=pl.BlockSpec((1, tm, d), lambda bi, i: (bi, i, 0)),
        scratch_shapes=[pltpu.VMEM((tm + CONV_HALO, d), F32),
                        pltpu.VMEM((SUBLANES - 1, tm + CONV_HALO - SUBLANES, d), F32),
                        pltpu.VMEM((tm, d), F32),
                        pltpu.VMEM((CONV_WIDTH + 1, SUBLANES, d), F32)],
        compiler_params=_cparams("parallel", "parallel"),
        name=name,
    )(z, z, dw.astype(F32), _row2(dw_b), _row2(ln_g), _row2(ln_b), w_out, _row2(b_out), x)


def _softplus(x):
    return jnp.maximum(x, 0.0) + jnp.log(1.0 + jnp.exp(-jnp.abs(x)))


def _rwkv_prep_kernel(x_ref, xh_ref, g_ref, mix_ref, wr_ref, wk_ref, wv_ref, w0_ref, w1_ref, w2_ref,
                      a0_ref, a1_ref, a2_ref, g1_ref, g2_ref, kk_ref, ka_ref, rk_ref, e_ref, e2_ref, et_ref,
                      e_o, an_o, b_o, k_o, wr_o, v_o, bv_o, g_o, s_o, hs_ref):
    tm = x_ref.shape[1]
    gn = g_ref[...]
    hn = _rms(x_ref[0], gn)
    not_first = (pl.program_id(1) != 0).astype(F32)
    hs_ref[0:SUBLANES, :] = _rms(xh_ref[0], gn) * not_first
    hs_ref[SUBLANES:, :] = hn
    delta = hs_ref[pl.ds(SUBLANES - 1, tm), :] - hn

    def mixed(c):
        return (hn + delta * mix_ref[c:c + 1, :]).astype(BF16)

    r = _dot(mixed(0), wr_ref[...])
    k = _dot(mixed(2), wk_ref[...])
    v = _dot(mixed(3), wv_ref[...])
    w_lora = _dot(jnp.tanh(_dot(mixed(1), w1_ref[...])).astype(BF16), w2_ref[...])
    w = -_softplus(-(w0_ref[...] + w_lora)) - 0.5
    e_neg_log = jnp.exp(w).astype(BF16)
    decay = jnp.exp(-e_neg_log.astype(F32))
    a = jax.nn.sigmoid(a0_ref[...] + _dot(_dot(mixed(4), a1_ref[...]).astype(BF16), a2_ref[...]))
    gate = _dot(jax.nn.sigmoid(_dot(mixed(5), g1_ref[...])).astype(BF16), g2_ref[...])

    e, e2, et = e_ref[...], e2_ref[...], et_ref[...]
    kk = k * kk_ref[...]
    ss = _dot(_head_sum(kk * kk, e).astype(BF16), et)
    kk = kk / jnp.maximum(jnp.sqrt(ss), 1e-12)
    k = k * (1.0 + (a - 1.0) * ka_ref[...])
    b = kk * a
    bonus = _dot(_head_sum(r * k * rk_ref[...], e).astype(BF16), et)

    e_o[0] = e_neg_log
    an_o[0] = (-kk).astype(an_o.dtype)
    b_o[0] = b.astype(b_o.dtype)
    k_o[0] = k.astype(k_o.dtype)
    wr_o[0] = (decay * r).astype(wr_o.dtype)
    v_o[0] = v.astype(v_o.dtype)
    bv_o[0] = (bonus * v).astype(bv_o.dtype)
    g_o[0] = gate.astype(g_o.dtype)
    s_o[0] = _head_sum(b * r, e) + _head_sum(k * r, e2)


def _rwkv_prep(x, g, p, e, e2, et, *, tm=256, name):
    b, t, d = x.shape
    assert t % tm == 0
    hb = tm // SUBLANES
    tile = pl.BlockSpec((1, tm, d), lambda bi, i: (bi, i, 0))
    vec = pl.BlockSpec((1, d), lambda bi, i: (0, 0))

    def full(a):
        return pl.BlockSpec(a.shape, lambda bi, i: (0,) * a.ndim)

    consts = [p["mix"], p["wr"], p["wk"], p["wv"], p["w0"], p["w1"], p["w2"], p["a0"], p["a1"], p["a2"],
              p["g1"], p["g2"], p["k_k"], p["k_a"], p["r_k"], e, e2, et]
    bf16_tile = jax.ShapeDtypeStruct((b, t, d), BF16)
    return pl.pallas_call(
        _rwkv_prep_kernel,
        out_shape=[bf16_tile] * 8 + [jax.ShapeDtypeStruct((b, t, LANES), F32)],
        grid=(b, t // tm),
        in_specs=[tile,
                  pl.BlockSpec((1, SUBLANES, d), lambda bi, i: (bi, jnp.maximum(i * hb - 1, 0), 0)),
                  vec] + [full(a) for a in consts],
        out_specs=[tile] * 8 + [pl.BlockSpec((1, tm, LANES), lambda bi, i: (bi, i, 0))],
        scratch_shapes=[pltpu.VMEM((tm + SUBLANES, d), F32)],
        compiler_params=_cparams("parallel", "parallel"),
        name=name,
    )(x, x, _row2(g), *consts)


def _rwkv_scan_kernel(e_ref, b_ref, k_ref, v_ref, sc_ref, an_ref, wr_ref, ann_ref, wrn_ref,
                      y_ref, s_ref, sa_ref, ys_ref, bs_ref, ks_ref, as_ref, ws_ref, pe_ref):
    tb, nk, nh = e_ref.shape
    nv = s_ref.shape[1]

    @pl.when(pl.program_id(0) == 0)
    def _():
        s_ref[...] = jnp.zeros_like(s_ref)
        sa_ref[...] = jnp.zeros_like(sa_ref)
        ys_ref[...] = jnp.zeros_like(ys_ref)

    cum = jnp.zeros((nk, nh), F32)
    for t in range(tb):
        cum = cum + e_ref[t].astype(F32)
        grow, shrink = jnp.exp(cum), jnp.exp(-cum)
        bs_ref[t] = b_ref[t].astype(F32) * grow
        ks_ref[t] = k_ref[t].astype(F32) * grow
        a_next = an_ref[t + 1] if t + 1 < tb else ann_ref[0]
        w_next = wr_ref[t + 1] if t + 1 < tb else wrn_ref[0]
        as_ref[t] = a_next.astype(F32) * shrink
        ws_ref[t] = w_next.astype(F32) * shrink
    pe_ref[...] = jnp.exp(-cum)

    def step(t, ln, last):
        sa, ys, v = sa_ref[:, ln], ys_ref[:, ln], v_ref[t, :, ln].astype(F32)
        y = ys + sa * sc_ref[t, 0:1, ln] + v * sc_ref[t, 1:2, ln]
        yc = y - jnp.sum(y, axis=0, keepdims=True) * (1.0 / nv)
        var = jnp.sum(yc * yc, axis=0, keepdims=True) * (1.0 / nv)
        y_ref[t, :, ln] = (yc * lax.rsqrt(var + RWKV_GN_EPS)).astype(y_ref.dtype)
        sa_n = jnp.zeros_like(sa)
        ys_n = jnp.zeros_like(sa)
        for j in range(nk):
            s = s_ref[j, :, ln] + sa * bs_ref[t, j:j + 1, ln] + v * ks_ref[t, j:j + 1, ln]
            s_ref[j, :, ln] = s * pe_ref[j:j + 1, ln] if last else s
            sa_n = sa_n + s * as_ref[t, j:j + 1, ln]
            ys_n = ys_n + s * ws_ref[t, j:j + 1, ln]
        sa_ref[:, ln] = sa_n
        ys_ref[:, ln] = ys_n

    for hb in range(nh // LANES):
        ln = slice(hb * LANES, (hb + 1) * LANES)

        def body(t, carry, ln=ln):
            step(t, ln, False)
            return carry

        lax.fori_loop(0, tb - 1, body, 0)
        step(tb - 1, ln, True)


def _rwkv_scan(e, an, bb, kk, wr, v, sc, *, tb=16, name):
    t, n, nh = e.shape
    assert t % tb == 0 and nh % LANES == 0
    blk = pl.BlockSpec((tb, n, nh), lambda i: (i, 0, 0))
    row = pl.BlockSpec((tb, 2, nh), lambda i: (i, 0, 0))
    nxt = pl.BlockSpec((1, n, nh), lambda i: (jnp.minimum((i + 1) * tb, t - 1), 0, 0))
    step_rows = pltpu.VMEM((tb, n, nh), F32)
    return pl.pallas_call(
        _rwkv_scan_kernel,
        out_shape=jax.ShapeDtypeStruct((t, n, nh), BF16),
        grid=(t // tb,),
        in_specs=[blk, blk, blk, blk, row, blk, blk, nxt, nxt],
        out_specs=blk,
        scratch_shapes=[pltpu.VMEM((n, n, nh), F32), pltpu.VMEM((n, nh), F32), pltpu.VMEM((n, nh), F32),
                        step_rows, step_rows, step_rows, step_rows, pltpu.VMEM((n, nh), F32)],
        compiler_params=_cparams("arbitrary"),
        name=name,
    )(e, bb, kk, v, sc, an, wr, an, wr)


def _rwkv_post_kernel(y_ref, bv_ref, g_ref, x_ref, gg_ref, gb_ref, w_ref, o_ref):
    z = ((y_ref[...].astype(F32) * gg_ref[...] + gb_ref[...] + bv_ref[...].astype(F32))
         * g_ref[...].astype(F32))
    o_ref[...] = x_ref[...] + _dot(z.astype(BF16), w_ref[...])


def _rwkv_post(y, bv, g, x2d, gn_g, gn_b, w_o, *, tm=1024, name):
    m, d = x2d.shape
    tile = pl.BlockSpec((tm, d), lambda i: (i, 0))
    vec = pl.BlockSpec((1, d), lambda i: (0, 0))
    return pl.pallas_call(
        _rwkv_post_kernel,
        out_shape=jax.ShapeDtypeStruct((m, d), F32),
        grid=(m // tm,),
        in_specs=[tile, tile, tile, tile, vec, vec, pl.BlockSpec((d, d), lambda i: (0, 0))],
        out_specs=tile,
        compiler_params=_cparams("parallel"),
        name=name,
    )(y, bv, g, x2d, _row2(gn_g), _row2(gn_b), w_o)


def _rwkv_mixer(x, g, p):
    b, t, d = x.shape
    n = RWKV_HEAD_DIM
    h = d // n
    lane = jnp.arange(LANES)[None, :]
    head = (jnp.arange(d) // n)[:, None]
    e = (head == lane).astype(BF16)
    e2 = (head + h == lane).astype(BF16)
    et = e.T
    el, an, bb, kk, wr, v, bv, gate, s = _rwkv_prep(x, g, p, e, e2, et, name="rwkv_prep")

    def tm(a):
        return a.reshape(b, t, h, n).transpose(1, 3, 0, 2).reshape(t, n, b * h)

    sc = s[:, :, :2 * h].reshape(b, t, 2, h).transpose(1, 2, 0, 3).reshape(t, 2, b * h)
    y = _rwkv_scan(tm(el), tm(an), tm(bb), tm(kk), tm(wr), tm(v), sc, name="rwkv_scan")
    y = y.reshape(t, n, b, h).transpose(2, 0, 3, 1).reshape(b * t, d)
    out = _rwkv_post(y, bv.reshape(b * t, d), gate.reshape(b * t, d), x.reshape(b * t, d),
                     p["gn_g"], p["gn_b"], p["w_o"], name="rwkv_post")
    return out.reshape(b, t, d)


def _sgu_tail_kernel(u_ref, v_ref, lg_ref, lb_ref, ws_ref, bst_ref, w_ref, x_ref, o_ref, vn_ref, gt_ref):
    tm, width = u_ref.shape[1], u_ref.shape[2]
    c = SGU_CHUNK
    gd = width // SGU_GROUPS
    vn_ref[...] = _layernorm(v_ref[0].astype(F32), lg_ref[...], lb_ref[...], LN_EPS).astype(BF16)
    tri = lax.broadcasted_iota(jnp.int32, (c, c), 0) >= lax.broadcasted_iota(jnp.int32, (c, c), 1)
    for gi in range(SGU_GROUPS):
        wsm = jnp.where(tri, ws_ref[gi], 0.0).astype(BF16)
        bias = bst_ref[:, gi:gi + 1]
        cols = slice(gi * gd, (gi + 1) * gd)
        for ci in range(tm // c):
            rows = slice(ci * c, (ci + 1) * c)
            mixed = _dot(wsm, vn_ref[rows, cols]) + bias
            gt_ref[rows, cols] = (u_ref[0, rows, cols].astype(F32) * mixed).astype(BF16)
    o_ref[0] = x_ref[0] + _dot(gt_ref[...], w_ref[...])


def _sgu_tail(uv, ln_g, ln_b, ws, bs, w_out, x, *, tm=512, name):
    b, t, d = x.shape
    width = uv.shape[-1] // 2
    assert t % tm == 0 and tm % SGU_CHUNK == 0
    return pl.pallas_call(
        _sgu_tail_kernel,
        out_shape=jax.ShapeDtypeStruct((b, t, d), F32),
        grid=(b, t // tm),
        in_specs=[pl.BlockSpec((1, tm, width), lambda bi, i: (bi, i, 0)),
                  pl.BlockSpec((1, tm, width), lambda bi, i: (bi, i, 1)),
                  pl.BlockSpec((1, width), lambda bi, i: (0, 0)),
                  pl.BlockSpec((1, width), lambda bi, i: (0, 0)),
                  pl.BlockSpec(ws.shape, lambda bi, i: (0, 0, 0)),
                  pl.BlockSpec((SGU_CHUNK, SGU_GROUPS), lambda bi, i: (0, 0)),
                  pl.BlockSpec((width, d), lambda bi, i: (0, 0)),
                  pl.BlockSpec((1, tm, d), lambda bi, i: (bi, i, 0))],
        out_specs=pl.BlockSpec((1, tm, d), lambda bi, i: (bi, i, 0)),
        scratch_shapes=[pltpu.VMEM((tm, width), BF16), pltpu.VMEM((tm, width), BF16)],
        compiler_params=_cparams("parallel", "parallel"),
        name=name,
    )(uv, uv, _row2(ln_g), _row2(ln_b), ws.astype(F32), bs.T.astype(F32), w_out, x)


def _ret_core_kernel(q_ref, k_ref, v_ref, g_ref, sin_ref, cos_ref, rot_ref, dm_ref, qd_ref, kd_ref, cd_ref,
                     o_ref, r_ref):
    @pl.when(pl.program_id(2) == 0)
    def _():
        r_ref[...] = jnp.zeros_like(r_ref)

    sin, cos, rot = sin_ref[...], cos_ref[...], rot_ref[...]
    qb, kb, vb = q_ref[0], k_ref[0], v_ref[0]
    q = qb.astype(F32) * cos + _dot(qb, rot) * sin
    k = (kb.astype(F32) * cos + _dot(kb, rot) * sin) * (qb.shape[-1] ** -0.5)
    scores = _dot_nt(q.astype(BF16), k.astype(BF16)) * dm_ref[0]
    inner = _dot(scores.astype(BF16), vb)
    r = r_ref[...]
    cross = _dot((q * qd_ref[0]).astype(BF16), r.astype(BF16))
    ks = (k * kd_ref[0]).astype(BF16)
    kv = lax.dot_general(ks, vb, (((0,), (0,)), ((), ())), preferred_element_type=F32)
    r_ref[...] = r * cd_ref[0, 0:1, 0:1] + kv
    o = inner + cross
    o = o * lax.rsqrt(jnp.mean(o * o, axis=-1, keepdims=True) + NORM_EPS)
    gate = g_ref[0].astype(F32)
    o_ref[0] = (gate * jax.nn.sigmoid(gate) * o).astype(o_ref.dtype)


def _ret_core(qkvg, t, d, *, name):
    b = qkvg.shape[0]
    hh, c = RET_HEADS, RET_CHUNK
    assert t % c == 0
    dk = d // hh
    dv = 2 * dk
    angle = jnp.repeat(1.0 / (10000.0 ** jnp.linspace(0.0, 1.0, dk // 2, dtype=F32)), 2)
    theta = jnp.arange(t, dtype=F32)[:, None] * angle[None]
    log_gamma = jnp.log(1.0 - 2.0 ** (-5.0 - jnp.arange(hh, dtype=F32)))
    idx = jnp.arange(c, dtype=F32)
    diff = idx[:, None] - idx[None, :]
    dm = jnp.where(diff[None] >= 0, jnp.exp(log_gamma[:, None, None] * jnp.maximum(diff, 0.0)[None]), 0.0)
    qd = jnp.exp(log_gamma[:, None] * (idx + 1.0))[:, :, None]
    kd = jnp.exp(log_gamma[:, None] * (c - 1.0 - idx))[:, :, None]
    cd = jnp.broadcast_to(jnp.exp(log_gamma * c)[:, None, None], (hh, SUBLANES, LANES))
    src, dst = jnp.arange(dk)[:, None], jnp.arange(dk)[None, :]
    rot = (jnp.where((dst % 2 == 1) & (src == dst - 1), 1.0, 0.0)
           - jnp.where((dst % 2 == 0) & (src == dst + 1), 1.0, 0.0)).astype(BF16)
    nq = d // dk
    nv = 2 * d // dv
    return pl.pallas_call(
        _ret_core_kernel,
        out_shape=jax.ShapeDtypeStruct((b, t, hh * dv), BF16),
        grid=(b, hh, t // c),
        in_specs=[pl.BlockSpec((1, c, dk), lambda bi, h, n: (bi, n, h)),
                  pl.BlockSpec((1, c, dk), lambda bi, h, n: (bi, n, nq + h)),
                  pl.BlockSpec((1, c, dv), lambda bi, h, n: (bi, n, nv + h)),
                  pl.BlockSpec((1, c, dv), lambda bi, h, n: (bi, n, nv + hh + h)),
                  pl.BlockSpec((c, dk), lambda bi, h, n: (n, 0)),
                  pl.BlockSpec((c, dk), lambda bi, h, n: (n, 0)),
                  pl.BlockSpec((dk, dk), lambda bi, h, n: (0, 0)),
                  pl.BlockSpec((1, c, c), lambda bi, h, n: (h, 0, 0)),
                  pl.BlockSpec((1, c, 1), lambda bi, h, n: (h, 0, 0)),
                  pl.BlockSpec((1, c, 1), lambda bi, h, n: (h, 0, 0)),
                  pl.BlockSpec((1, SUBLANES, LANES), lambda bi, h, n: (h, 0, 0))],
        out_specs=pl.BlockSpec((1, c, dv), lambda bi, h, n: (bi, n, h)),
        scratch_shapes=[pltpu.VMEM((dk, dv), F32)],
        compiler_params=_cparams("parallel", "parallel", "arbitrary"),
        name=name,
    )(qkvg, qkvg, qkvg, qkvg, jnp.sin(theta), jnp.cos(theta), rot, dm, qd, kd, cd)


def _pad_axis(a, axis, to):
    pad = [(0, 0)] * a.ndim
    pad[axis] = (0, to - a.shape[axis])
    return jnp.pad(a, pad)


def kernel(x, mem, mix_norm_g, xattn_norm_g, mem_norm_g, xattn_wq, xattn_wkv, xattn_wo, ffn_norm_g, ffn_w_gate_up, ffn_w_down, conv_w_in, conv_b_in, conv_dw, conv_dw_b, conv_ln_g, conv_ln_b, conv_w_out, conv_b_out, rwkv_mix, rwkv_w_rkv, rwkv_w0, rwkv_w1, rwkv_w2, rwkv_a0, rwkv_a1, rwkv_a2, rwkv_g1, rwkv_g2, rwkv_k_k, rwkv_k_a, rwkv_r_k, rwkv_gn_g, rwkv_gn_b, rwkv_w_o, sgu_w_in, sgu_b_in, sgu_ln_g, sgu_ln_b, sgu_ws, sgu_bs, sgu_w_out, ret_w_in, ret_w_out, final_norm_g):
    b, t, d = x.shape
    n_mem = mem.shape[1]
    depth = mix_norm_g.shape[0]
    m = b * t
    mem2d = mem.reshape(b * n_mem, d)

    for i in range(depth):
        mixer, j = i % 4, i // 4
        g = mix_norm_g[i]
        if mixer == 0:
            z = _norm_mm(x.reshape(m, d), g, conv_w_in[j].astype(BF16), conv_b_in[j], glu=True,
                         name="conv_in")
            x = _conv_tail(z.reshape(b, t, d), conv_dw[j], conv_dw_b[j], conv_ln_g[j], conv_ln_b[j],
                           conv_w_out[j].astype(BF16), conv_b_out[j], x, name="conv_tail")
        elif mixer == 1:
            lw = _pad_axis(rwkv_w1[j], 1, LANES)
            la = _pad_axis(rwkv_a1[j], 1, LANES)
            lg = _pad_axis(rwkv_g1[j], 1, 2 * LANES)
            p = dict(mix=rwkv_mix[j].astype(F32),
                     wr=rwkv_w_rkv[j, 0].astype(BF16), wk=rwkv_w_rkv[j, 1].astype(BF16),
                     wv=rwkv_w_rkv[j, 2].astype(BF16),
                     w0=_row2(rwkv_w0[j]), w1=lw.astype(BF16),
                     w2=_pad_axis(rwkv_w2[j], 0, LANES).astype(BF16),
                     a0=_row2(rwkv_a0[j]), a1=la.astype(BF16),
                     a2=_pad_axis(rwkv_a2[j], 0, LANES).astype(BF16),
                     g1=lg.astype(BF16), g2=_pad_axis(rwkv_g2[j], 0, 2 * LANES).astype(BF16),
                     k_k=_row2(rwkv_k_k[j]), k_a=_row2(rwkv_k_a[j]), r_k=_row2(rwkv_r_k[j]),
                     gn_g=rwkv_gn_g[j], gn_b=rwkv_gn_b[j], w_o=rwkv_w_o[j].astype(BF16))
            x = _rwkv_mixer(x, g, p)
        elif mixer == 2:
            uv = _norm_mm(x.reshape(m, d), g, sgu_w_in[j].astype(BF16), sgu_b_in[j], act="gelu",
                          name="sgu_in")
            x = _sgu_tail(uv.reshape(b, t, -1), sgu_ln_g[j], sgu_ln_b[j], sgu_ws[j], sgu_bs[j],
                          sgu_w_out[j].astype(BF16), x, name="sgu_tail")
        else:
            qkvg = _norm_mm(x.reshape(m, d), g, ret_w_in[j].astype(BF16), name="ret_in")
            o = _ret_core(qkvg.reshape(b, t, -1), t, d, name="ret_core")
            x = _mm_res(o.reshape(m, -1), ret_w_out[j].astype(BF16), x.reshape(m, d),
                        name="ret_out").reshape(b, t, d)

        kv = _norm_mm(mem2d, mem_norm_g[i], xattn_wkv[i].astype(BF16), name="xattn_kv")
        x = _xattn(x, xattn_norm_g[i], xattn_wq[i].astype(BF16), kv.reshape(b, n_mem, 2 * d),
                   xattn_wo[i].astype(BF16), name="xattn")
        fg = final_norm_g if i == depth - 1 else None
        x = _swiglu(x.reshape(m, d), ffn_norm_g[i], ffn_w_gate_up[i].astype(BF16),
                    ffn_w_down[i].astype(BF16), fg, name="swiglu").reshape(b, t, d)
    return x
```

```python
import functools
import math

import jax
import jax.numpy as jnp
from jax import lax
from jax.experimental import pallas as pl
from jax.experimental.pallas import tpu as pltpu

F32 = jnp.float32
BF16 = jnp.bfloat16

NORM_EPS = 1e-6
LN_EPS = 1e-5

XA_HEADS = 4
CONV_WIDTH = 31
CONV_HALO = 32
RWKV_HEAD_DIM = 64
RWKV_GN_EPS = RWKV_HEAD_DIM * 1e-5
SGU_CHUNK = 128
SGU_GROUPS = 8
RET_HEADS = 4
RET_CHUNK = 512

LANES = 128
MXU_WIDTH = 256
SUBLANES = 8
VMEM_LIMIT_BYTES = 56 * 1024 * 1024


def _cparams(*sem):
    return pltpu.CompilerParams(dimension_semantics=sem, vmem_limit_bytes=VMEM_LIMIT_BYTES)


def _rms(x, g):
    return x * lax.rsqrt(jnp.mean(x * x, axis=-1, keepdims=True) + NORM_EPS) * g


def _layernorm(x, g, b, eps):
    mu = jnp.mean(x, axis=-1, keepdims=True)
    xc = x - mu
    var = jnp.mean(xc * xc, axis=-1, keepdims=True)
    return xc * lax.rsqrt(var + eps) * g + b


def _dot(a, b):
    return jnp.dot(a, b, preferred_element_type=F32)


def _dot_nt(a, b):
    return lax.dot_general(a, b, (((1,), (1,)), ((), ())), preferred_element_type=F32)


def _head_sum(x, e):
    return _dot(x.astype(BF16), e)


def _row2(v):
    return v.reshape(1, -1).astype(F32)


def _norm_mm_kernel(*refs, act, glu, has_bias, chunks):
    if has_bias:
        x_ref, g_ref, w_ref, b_ref, o_ref = refs
    else:
        x_ref, g_ref, w_ref, o_ref = refs
    xn = _rms(x_ref[...], g_ref[...]).astype(BF16)
    n_out = o_ref.shape[1]
    for lo, hi in chunks:
        y = _dot(xn, w_ref[:, lo:hi])
        if has_bias:
            y = y + b_ref[:, lo:hi]
        if glu:
            y2 = _dot(xn, w_ref[:, n_out + lo:n_out + hi])
            if has_bias:
                y2 = y2 + b_ref[:, n_out + lo:n_out + hi]
            y = y * jax.nn.sigmoid(y2)
        elif act == "gelu":
            y = 0.5 * y * (1.0 + lax.erf(y * math.sqrt(0.5)))
        o_ref[:, lo:hi] = y.astype(o_ref.dtype)


def _norm_mm(x2d, g, w, bias=None, *, act=None, glu=False, out_dtype=BF16, tm=512, chunk=4 * MXU_WIDTH, name):
    m, k = x2d.shape
    n_w = w.shape[1]
    n_out = n_w // 2 if glu else n_w
    assert m % tm == 0 and n_out % MXU_WIDTH == 0
    chunks = tuple((lo, min(lo + chunk, n_out)) for lo in range(0, n_out, chunk))
    has_bias = bias is not None
    resident = dict(pipeline_mode=pl.Buffered(1))
    args = [x2d, _row2(g), w]
    specs = [pl.BlockSpec((tm, k), lambda i: (i, 0)),
             pl.BlockSpec((1, k), lambda i: (0, 0)),
             pl.BlockSpec((k, n_w), lambda i: (0, 0), **resident)]
    if has_bias:
        args.append(_row2(bias))
        specs.append(pl.BlockSpec((1, n_w), lambda i: (0, 0)))
    return pl.pallas_call(
        functools.partial(_norm_mm_kernel, act=act, glu=glu, has_bias=has_bias, chunks=chunks),
        out_shape=jax.ShapeDtypeStruct((m, n_out), out_dtype),
        grid=(m // tm,),
        in_specs=specs,
        out_specs=pl.BlockSpec((tm, n_out), lambda i: (i, 0)),
        compiler_params=_cparams("parallel"),
        name=name,
    )(*args)


def _mm_res_kernel(a_ref, w_ref, r_ref, o_ref):
    o_ref[...] = r_ref[...] + _dot(a_ref[...], w_ref[...])


def _mm_res(a, w, res, *, tm=1024, name):
    m, k = a.shape
    n = w.shape[1]
    assert m % tm == 0
    return pl.pallas_call(
        _mm_res_kernel,
        out_shape=jax.ShapeDtypeStruct((m, n), F32),
        grid=(m // tm,),
        in_specs=[pl.BlockSpec((tm, k), lambda i: (i, 0)),
                  pl.BlockSpec((k, n), lambda i: (0, 0)),
                  pl.BlockSpec((tm, n), lambda i: (i, 0))],
        out_specs=pl.BlockSpec((tm, n), lambda i: (i, 0)),
        compiler_params=_cparams("parallel"),
        name=name,
    )(a, w, res)


def _xattn_kernel(x_ref, g_ref, wq_ref, kv_ref, wo_ref, o_ref, oh_ref, *, heads):
    x = x_ref[0]
    d = x.shape[-1]
    hd = d // heads
    xn = _rms(x, g_ref[...]).astype(BF16)
    q = _dot(xn, wq_ref[...]).astype(BF16)
    scores = [_dot_nt(q[:, h * hd:(h + 1) * hd], kv_ref[0, :, h * hd:(h + 1) * hd]) * (hd ** -0.5)
              for h in range(heads)]
    probs = []
    for s in scores:
        p = jnp.exp(s - jnp.max(s, axis=-1, keepdims=True))
        probs.append((p / jnp.sum(p, axis=-1, keepdims=True)).astype(BF16))
    for h in range(heads):
        oh_ref[:, h * hd:(h + 1) * hd] = _dot(probs[h], kv_ref[0, :, d + h * hd:d + (h + 1) * hd]).astype(BF16)
    o_ref[0] = x + _dot(oh_ref[...], wo_ref[...])


def _xattn(x, g, wq, kv, wo, *, tm=1024, name):
    b, t, d = x.shape
    m = kv.shape[1]
    assert t % tm == 0
    return pl.pallas_call(
        functools.partial(_xattn_kernel, heads=XA_HEADS),
        out_shape=jax.ShapeDtypeStruct((b, t, d), F32),
        grid=(b, t // tm),
        in_specs=[pl.BlockSpec((1, tm, d), lambda bi, i: (bi, i, 0)),
                  pl.BlockSpec((1, d), lambda bi, i: (0, 0)),
                  pl.BlockSpec((d, d), lambda bi, i: (0, 0)),
                  pl.BlockSpec((1, m, 2 * d), lambda bi, i: (bi, 0, 0)),
                  pl.BlockSpec((d, d), lambda bi, i: (0, 0))],
        out_specs=pl.BlockSpec((1, tm, d), lambda bi, i: (bi, i, 0)),
        scratch_shapes=[pltpu.VMEM((tm, d), BF16)],
        compiler_params=_cparams("parallel", "parallel"),
        name=name,
    )(x, _row2(g), wq, kv, wo)


def _swiglu_kernel(*refs, final, chunks):
    if final:
        x_ref, g_ref, wg_ref, wu_ref, wd_ref, fg_ref, o_ref = refs
    else:
        x_ref, g_ref, wg_ref, wu_ref, wd_ref, o_ref = refs
    x = x_ref[...]
    xn = _rms(x, g_ref[...]).astype(BF16)
    y = x
    for lo, hi in chunks:
        gt = _dot(xn, wg_ref[:, lo:hi])
        up = _dot(xn, wu_ref[:, lo:hi])
        h = (gt * jax.nn.sigmoid(gt) * up).astype(BF16)
        y = y + _dot(h, wd_ref[lo:hi, :])
    if final:
        y = _rms(y, fg_ref[...])
    o_ref[...] = y


def _swiglu(x2d, g, w_gate_up, w_down, final_g=None, *, tm=512, chunk=3 * MXU_WIDTH, name):
    m, d = x2d.shape
    ff = w_down.shape[0]
    assert ff % MXU_WIDTH == 0 and m % tm == 0
    chunks = tuple((lo, min(lo + chunk, ff)) for lo in range(0, ff, chunk))
    final = final_g is not None
    resident = dict(pipeline_mode=pl.Buffered(1))
    args = [x2d, _row2(g), w_gate_up, w_gate_up, w_down]
    specs = [pl.BlockSpec((tm, d), lambda i: (i, 0)),
             pl.BlockSpec((1, d), lambda i: (0, 0)),
             pl.BlockSpec((d, ff), lambda i: (0, 0), **resident),
             pl.BlockSpec((d, ff), lambda i: (0, 1), **resident),
             pl.BlockSpec((ff, d), lambda i: (0, 0), **resident)]
    if final:
        args.append(_row2(final_g))
        specs.append(pl.BlockSpec((1, d), lambda i: (0, 0)))
    return pl.pallas_call(
        functools.partial(_swiglu_kernel, final=final, chunks=chunks),
        out_shape=jax.ShapeDtypeStruct((m, d), F32),
        grid=(m // tm,),
        in_specs=specs,
        out_specs=pl.BlockSpec((tm, d), lambda i: (i, 0)),
        compiler_params=_cparams("parallel"),
        name=name,
    )(*args)


def _conv_tail_kernel(z_ref, zh_ref, dw_ref, dwb_ref, lg_ref, lb_ref, w_ref, b_ref, x_ref,
                      o_ref, zz_ref, sh_ref, c_ref, dwx_ref, *, rows):
    tm, d = z_ref.shape[1], z_ref.shape[2]
    not_first = (pl.program_id(1) != 0).astype(F32)
    zz_ref[0:CONV_HALO, :] = zh_ref[0].astype(F32) * not_first
    zz_ref[CONV_HALO:, :] = z_ref[0].astype(F32)
    for s in range(1, SUBLANES):
        sh_ref[s - 1] = zz_ref[pl.ds(s, sh_ref.shape[1]), :]
    for k in range(CONV_WIDTH):
        dwx_ref[k] = jnp.broadcast_to(dw_ref[k:k + 1, :], (SUBLANES, d))
    dwx_ref[CONV_WIDTH] = jnp.broadcast_to(dwb_ref[...], (SUBLANES, d))
    base = CONV_HALO - (CONV_WIDTH - 1)
    groups = rows // SUBLANES

    def chunk(c, carry):
        r0 = pl.multiple_of(c * rows, rows)
        acc = [dwx_ref[CONV_WIDTH]] * groups
        for k in range(CONV_WIDTH):
            blk, s = divmod(base + k, SUBLANES)
            src = zz_ref if s == 0 else sh_ref.at[s - 1]
            wk = dwx_ref[k]
            for gi in range(groups):
                acc[gi] = acc[gi] + src[pl.ds(r0 + (blk + gi) * SUBLANES, SUBLANES), :] * wk
        for gi in range(groups):
            c_ref[pl.ds(r0 + gi * SUBLANES, SUBLANES), :] = acc[gi]
        return carry

    lax.fori_loop(0, tm // rows, chunk, 0)
    y = _layernorm(c_ref[...], lg_ref[...], lb_ref[...], LN_EPS)
    act = (y * jax.nn.sigmoid(y)).astype(BF16)
    o_ref[0] = x_ref[0] + _dot(act, w_ref[...]) + b_ref[...]


def _conv_tail(z, dw, dw_b, ln_g, ln_b, w_out, b_out, x, *, tm=512, rows=32, name):
    b, t, d = x.shape
    assert t % tm == 0 and tm % rows == 0 and rows % SUBLANES == 0
    hb = tm // CONV_HALO
    vec = pl.BlockSpec((1, d), lambda bi, i: (0, 0))
    return pl.pallas_call(
        functools.partial(_conv_tail_kernel, rows=rows),
        out_shape=jax.ShapeDtypeStruct((b, t, d), F32),
        grid=(b, t // tm),
        in_specs=[pl.BlockSpec((1, tm, d), lambda bi, i: (bi, i, 0)),
                  pl.BlockSpec((1, CONV_HALO, d), lambda bi, i: (bi, jnp.maximum(i * hb - 1, 0), 0)),
                  pl.BlockSpec((CONV_WIDTH, d), lambda bi, i: (0, 0)),
                  vec, vec, vec,
                  pl.BlockSpec((d, d), lambda bi, i: (0, 0)),
                  vec,
                  pl.BlockSpec((1, tm, d), lambda bi, i: (bi, i, 0))],
        out_specs=pl.BlockSpec((1, tm, d), lambda bi, i: (bi, i, 0)),
        scratch_shapes=[pltpu.VMEM((tm + CONV_HALO, d), F32),
                        pltpu.VMEM((SUBLANES - 1, tm + CONV_HALO - SUBLANES, d), F32),
                        pltpu.VMEM((tm, d), F32),
                        pltpu.VMEM((CONV_WIDTH + 1, SUBLANES, d), F32)],
        compiler_params=_cparams("parallel", "parallel"),
        name=name,
    )(z, z, dw.astype(F32), _row2(dw_b), _row2(ln_g), _row2(ln_b), w_out, _row2(b_out), x)


def _softplus(x):
    return jnp.maximum(x, 0.0) + jnp.log(1.0 + jnp.exp(-jnp.abs(x)))


def _rwkv_prep_kernel(x_ref, xh_ref, g_ref, mix_ref, wr_ref, wk_ref, wv_ref, w0_ref, w1_ref, w2_ref,
                      a0_ref, a1_ref, a2_ref, g1_ref, g2_ref, kk_ref, ka_ref, rk_ref, e_ref, e2_ref, et_ref,
                      e_o, an_o, b_o, k_o, wr_o, v_o, bv_o, g_o, s_o, hs_ref):
    tm = x_ref.shape[1]
    gn = g_ref[...]
    hn = _rms(x_ref[0], gn)
    not_first = (pl.program_id(1) != 0).astype(F32)
    hs_ref[0:SUBLANES, :] = _rms(xh_ref[0], gn) * not_first
    hs_ref[SUBLANES:, :] = hn
    delta = hs_ref[pl.ds(SUBLANES - 1, tm), :] - hn

    def mixed(c):
        return (hn + delta * mix_ref[c:c + 1, :]).astype(BF16)

    r = _dot(mixed(0), wr_ref[...])
    k = _dot(mixed(2), wk_ref[...])
    v = _dot(mixed(3), wv_ref[...])
    w_lora = _dot(jnp.tanh(_dot(mixed(1), w1_ref[...])).astype(BF16), w2_ref[...])
    w = -_softplus(-(w0_ref[...] + w_lora)) - 0.5
    e_neg_log = jnp.exp(w).astype(BF16)
    decay = jnp.exp(-e_neg_log.astype(F32))
    a = jax.nn.sigmoid(a0_ref[...] + _dot(_dot(mixed(4), a1_ref[...]).astype(BF16), a2_ref[...]))
    gate = _dot(jax.nn.sigmoid(_dot(mixed(5), g1_ref[...])).astype(BF16), g2_ref[...])

    e, e2, et = e_ref[...], e2_ref[...], et_ref[...]
    kk = k * kk_ref[...]
    ss = _dot(_head_sum(kk * kk, e).astype(BF16), et)
    kk = kk / jnp.maximum(jnp.sqrt(ss), 1e-12)
    k = k * (1.0 + (a - 1.0) * ka_ref[...])
    b = kk * a
    bonus = _dot(_head_sum(r * k * rk_ref[...], e).astype(BF16), et)

    e_o[0] = e_neg_log
    an_o[0] = (-kk).astype(an_o.dtype)
    b_o[0] = b.astype(b_o.dtype)
    k_o[0] = k.astype(k_o.dtype)
    wr_o[0] = (decay * r).astype(wr_o.dtype)
    v_o[0] = v.astype(v_o.dtype)
    bv_o[0] = (bonus * v).astype(bv_o.dtype)
    g_o[0] = gate.astype(g_o.dtype)
    s_o[0] = _head_sum(b * r, e) + _head_sum(k * r, e2)


def _rwkv_prep(x, g, p, e, e2, et, *, tm=256, name):
    b, t, d = x.shape
    assert t % tm == 0
    hb = tm // SUBLANES
    tile = pl.BlockSpec((1, tm, d), lambda bi, i: (bi, i, 0))
    vec = pl.BlockSpec((1, d), lambda bi, i: (0, 0))

    def full(a):
        return pl.BlockSpec(a.shape, lambda bi, i: (0,) * a.ndim)

    consts = [p["mix"], p["wr"], p["wk"], p["wv"], p["w0"], p["w1"], p["w2"], p["a0"], p["a1"], p["a2"],
              p["g1"], p["g2"], p["k_k"], p["k_a"], p["r_k"], e, e2, et]
    bf16_tile = jax.ShapeDtypeStruct((b, t, d), BF16)
    return pl.pallas_call(
        _rwkv_prep_kernel,
        out_shape=[bf16_tile] * 8 + [jax.ShapeDtypeStruct((b, t, LANES), F32)],
        grid=(b, t // tm),
        in_specs=[tile,
                  pl.BlockSpec((1, SUBLANES, d), lambda bi, i: (bi, jnp.maximum(i * hb - 1, 0), 0)),
                  vec] + [full(a) for a in consts],
        out_specs=[tile] * 8 + [pl.BlockSpec((1, tm, LANES), lambda bi, i: (bi, i, 0))],
        scratch_shapes=[pltpu.VMEM((tm + SUBLANES, d), F32)],
        compiler_params=_cparams("parallel", "parallel"),
        name=name,
    )(x, x, _row2(g), *consts)


def _rwkv_scan_kernel(e_ref, b_ref, k_ref, v_ref, sc_ref, an_ref, wr_ref, ann_ref, wrn_ref,
                      y_ref, s_ref, sa_ref, ys_ref, bs_ref, ks_ref, as_ref, ws_ref, pe_ref):
    tb, nk, nh = e_ref.shape
    nv = s_ref.shape[1]

    @pl.when(pl.program_id(0) == 0)
    def _():
        s_ref[...] = jnp.zeros_like(s_ref)
        sa_ref[...] = jnp.zeros_like(sa_ref)
        ys_ref[...] = jnp.zeros_like(ys_ref)

    cum = jnp.zeros((nk, nh), F32)
    for t in range(tb):
        cum = cum + e_ref[t].astype(F32)
        grow, shrink = jnp.exp(cum), jnp.exp(-cum)
        bs_ref[t] = b_ref[t].astype(F32) * grow
        ks_ref[t] = k_ref[t].astype(F32) * grow
        a_next = an_ref[t + 1] if t + 1 < tb else ann_ref[0]
        w_next = wr_ref[t + 1] if t + 1 < tb else wrn_ref[0]
        as_ref[t] = a_next.astype(F32) * shrink
        ws_ref[t] = w_next.astype(F32) * shrink
    pe_ref[...] = jnp.exp(-cum)

    def step(t, ln, last):
        sa, ys, v = sa_ref[:, ln], ys_ref[:, ln], v_ref[t, :, ln].astype(F32)
        y = ys + sa * sc_ref[t, 0:1, ln] + v * sc_ref[t, 1:2, ln]
        yc = y - jnp.sum(y, axis=0, keepdims=True) * (1.0 / nv)
        var = jnp.sum(yc * yc, axis=0, keepdims=True) * (1.0 / nv)
        y_ref[t, :, ln] = (yc * lax.rsqrt(var + RWKV_GN_EPS)).astype(y_ref.dtype)
        sa_n = jnp.zeros_like(sa)
        ys_n = jnp.zeros_like(sa)
        for j in range(nk):
            s = s_ref[j, :, ln] + sa * bs_ref[t, j:j + 1, ln] + v * ks_ref[t, j:j + 1, ln]
            s_ref[j, :, ln] = s * pe_ref[j:j + 1, ln] if last else s
            sa_n = sa_n + s * as_ref[t, j:j + 1, ln]
            ys_n = ys_n + s * ws_ref[t, j:j + 1, ln]
        sa_ref[:, ln] = sa_n
        ys_ref[:, ln] = ys_n

    for hb in range(nh // LANES):
        ln = slice(hb * LANES, (hb + 1) * LANES)

        def body(t, carry, ln=ln):
            step(t, ln, False)
            return carry

        lax.fori_loop(0, tb - 1, body, 0)
        step(tb - 1, ln, True)


def _rwkv_scan(e, an, bb, kk, wr, v, sc, *, tb=32, name):
    t, n, nh = e.shape
    assert t % tb == 0 and nh % LANES == 0
    blk = pl.BlockSpec((tb, n, nh), lambda i: (i, 0, 0))
    row = pl.BlockSpec((tb, 2, nh), lambda i: (i, 0, 0))
    nxt = pl.BlockSpec((1, n, nh), lambda i: (jnp.minimum((i + 1) * tb, t - 1), 0, 0))
    step_rows = pltpu.VMEM((tb, n, nh), F32)
    return pl.pallas_call(
        _rwkv_scan_kernel,
        out_shape=jax.ShapeDtypeStruct((t, n, nh), BF16),
        grid=(t // tb,),
        in_specs=[blk, blk, blk, blk, row, blk, blk, nxt, nxt],
        out_specs=blk,
        scratch_shapes=[pltpu.VMEM((n, n, nh), F32), pltpu.VMEM((n, nh), F32), pltpu.VMEM((n, nh), F32),
                        step_rows, step_rows, step_rows, step_rows, pltpu.VMEM((n, nh), F32)],
        compiler_params=_cparams("arbitrary"),
        name=name,
    )(e, bb, kk, v, sc, an, wr, an, wr)


def _rwkv_post_kernel(y_ref, bv_ref, g_ref, x_ref, gg_ref, gb_ref, w_ref, o_ref):
    z = ((y_ref[...].astype(F32) * gg_ref[...] + gb_ref[...] + bv_ref[...].astype(F32))
         * g_ref[...].astype(F32))
    o_ref[...] = x_ref[...] + _dot(z.astype(BF16), w_ref[...])


def _rwkv_post(y, bv, g, x2d, gn_g, gn_b, w_o, *, tm=1024, name):
    m, d = x2d.shape
    tile = pl.BlockSpec((tm, d), lambda i: (i, 0))
    vec = pl.BlockSpec((1, d), lambda i: (0, 0))
    return pl.pallas_call(
        _rwkv_post_kernel,
        out_shape=jax.ShapeDtypeStruct((m, d), F32),
        grid=(m // tm,),
        in_specs=[tile, tile, tile, tile, vec, vec, pl.BlockSpec((d, d), lambda i: (0, 0))],
        out_specs=tile,
        compiler_params=_cparams("parallel"),
        name=name,
    )(y, bv, g, x2d, _row2(gn_g), _row2(gn_b), w_o)


def _rwkv_mixer(x, g, p):
    b, t, d = x.shape
    n = RWKV_HEAD_DIM
    h = d // n
    lane = jnp.arange(LANES)[None, :]
    head = (jnp.arange(d) // n)[:, None]
    e = (head == lane).astype(BF16)
    e2 = (head + h == lane).astype(BF16)
    et = e.T
    el, an, bb, kk, wr, v, bv, gate, s = _rwkv_prep(x, g, p, e, e2, et, name="rwkv_prep")

    def tm(a):
        return a.reshape(b, t, h, n).transpose(1, 3, 0, 2).reshape(t, n, b * h)

    sc = s[:, :, :2 * h].reshape(b, t, 2, h).transpose(1, 2, 0, 3).reshape(t, 2, b * h)
    y = _rwkv_scan(tm(el), tm(an), tm(bb), tm(kk), tm(wr), tm(v), sc, name="rwkv_scan")
    y = y.reshape(t, n, b, h).transpose(2, 0, 3, 1).reshape(b * t, d)
    out = _rwkv_post(y, bv.reshape(b * t, d), gate.reshape(b * t, d), x.reshape(b * t, d),
                     p["gn_g"], p["gn_b"], p["w_o"], name="rwkv_post")
    return out.reshape(b, t, d)


def _sgu_tail_kernel(u_ref, v_ref, lg_ref, lb_ref, ws_ref, bst_ref, w_ref, x_ref, o_ref, vn_ref, gt_ref):
    tm, width = u_ref.shape[1], u_ref.shape[2]
    c = SGU_CHUNK
    gd = width // SGU_GROUPS
    vn_ref[...] = _layernorm(v_ref[0].astype(F32), lg_ref[...], lb_ref[...], LN_EPS).astype(BF16)
    tri = lax.broadcasted_iota(jnp.int32, (c, c), 0) >= lax.broadcasted_iota(jnp.int32, (c, c), 1)
    for gi in range(SGU_GROUPS):
        wsm = jnp.where(tri, ws_ref[gi], 0.0).astype(BF16)
        bias = bst_ref[:, gi:gi + 1]
        cols = slice(gi * gd, (gi + 1) * gd)
        for ci in range(tm // c):
            rows = slice(ci * c, (ci + 1) * c)
            mixed = _dot(wsm, vn_ref[rows, cols]) + bias
            gt_ref[rows, cols] = (u_ref[0, rows, cols].astype(F32) * mixed).astype(BF16)
    o_ref[0] = x_ref[0] + _dot(gt_ref[...], w_ref[...])


def _sgu_tail(uv, ln_g, ln_b, ws, bs, w_out, x, *, tm=512, name):
    b, t, d = x.shape
    width = uv.shape[-1] // 2
    assert t % tm == 0 and tm % SGU_CHUNK == 0
    return pl.pallas_call(
        _sgu_tail_kernel,
        out_shape=jax.ShapeDtypeStruct((b, t, d), F32),
        grid=(b, t // tm),
        in_specs=[pl.BlockSpec((1, tm, width), lambda bi, i: (bi, i, 0)),
                  pl.BlockSpec((1, tm, width), lambda bi, i: (bi, i, 1)),
                  pl.BlockSpec((1, width), lambda bi, i: (0, 0)),
                  pl.BlockSpec((1, width), lambda bi, i: (0, 0)),
                  pl.BlockSpec(ws.shape, lambda bi, i: (0, 0, 0)),
                  pl.BlockSpec((SGU_CHUNK, SGU_GROUPS), lambda bi, i: (0, 0)),
                  pl.BlockSpec((width, d), lambda bi, i: (0, 0)),
                  pl.BlockSpec((1, tm, d), lambda bi, i: (bi, i, 0))],
        out_specs=pl.BlockSpec((1, tm, d), lambda bi, i: (bi, i, 0)),
        scratch_shapes=[pltpu.VMEM((tm, width), BF16), pltpu.VMEM((tm, width), BF16)],
        compiler_params=_cparams("parallel", "parallel"),
        name=name,
    )(uv, uv, _row2(ln_g), _row2(ln_b), ws.astype(F32), bs.T.astype(F32), w_out, x)


def _ret_core_kernel(q_ref, k_ref, v_ref, g_ref, sin_ref, cos_ref, rot_ref, dm_ref, qd_ref, kd_ref, cd_ref,
                     o_ref, r_ref):
    @pl.when(pl.program_id(2) == 0)
    def _():
        r_ref[...] = jnp.zeros_like(r_ref)

    sin, cos, rot = sin_ref[...], cos_ref[...], rot_ref[...]
    qb, kb, vb = q_ref[0], k_ref[0], v_ref[0]
    q = qb.astype(F32) * cos + _dot(qb, rot) * sin
    k = (kb.astype(F32) * cos + _dot(kb, rot) * sin) * (qb.shape[-1] ** -0.5)
    half = qb.shape[0] // 2
    q16, k16 = q.astype(BF16), k.astype(BF16)
    s_ee = (_dot_nt(q16[:half], k16[:half]) * dm_ref[0, :half, :half]).astype(BF16)
    s_le = (_dot_nt(q16[half:], k16[:half]) * dm_ref[0, half:, :half]).astype(BF16)
    s_ll = (_dot_nt(q16[half:], k16[half:]) * dm_ref[0, half:, half:]).astype(BF16)
    inner = jnp.concatenate([_dot(s_ee, vb[:half]), _dot(s_le, vb[:half]) + _dot(s_ll, vb[half:])], axis=0)
    r = r_ref[...]
    cross = _dot((q * qd_ref[0]).astype(BF16), r.astype(BF16))
    ks = (k * kd_ref[0]).astype(BF16)
    kv = lax.dot_general(ks, vb, (((0,), (0,)), ((), ())), preferred_element_type=F32)
    r_ref[...] = r * cd_ref[0, 0:1, 0:1] + kv
    o = inner + cross
    o = o * lax.rsqrt(jnp.mean(o * o, axis=-1, keepdims=True) + NORM_EPS)
    gate = g_ref[0].astype(F32)
    o_ref[0] = (gate * jax.nn.sigmoid(gate) * o).astype(o_ref.dtype)


def _ret_core(qkvg, t, d, *, name):
    b = qkvg.shape[0]
    hh, c = RET_HEADS, RET_CHUNK
    assert t % c == 0
    dk = d // hh
    dv = 2 * dk
    angle = jnp.repeat(1.0 / (10000.0 ** jnp.linspace(0.0, 1.0, dk // 2, dtype=F32)), 2)
    theta = jnp.arange(t, dtype=F32)[:, None] * angle[None]
    log_gamma = jnp.log(1.0 - 2.0 ** (-5.0 - jnp.arange(hh, dtype=F32)))
    idx = jnp.arange(c, dtype=F32)
    diff = idx[:, None] - idx[None, :]
    dm = jnp.where(diff[None] >= 0, jnp.exp(log_gamma[:, None, None] * jnp.maximum(diff, 0.0)[None]), 0.0)
    qd = jnp.exp(log_gamma[:, None] * (idx + 1.0))[:, :, None]
    kd = jnp.exp(log_gamma[:, None] * (c - 1.0 - idx))[:, :, None]
    cd = jnp.broadcast_to(jnp.exp(log_gamma * c)[:, None, None], (hh, SUBLANES, LANES))
    src, dst = jnp.arange(dk)[:, None], jnp.arange(dk)[None, :]
    rot = (jnp.where((dst % 2 == 1) & (src == dst - 1), 1.0, 0.0)
           - jnp.where((dst % 2 == 0) & (src == dst + 1), 1.0, 0.0)).astype(BF16)
    nq = d // dk
    nv = 2 * d // dv
    return pl.pallas_call(
        _ret_core_kernel,
        out_shape=jax.ShapeDtypeStruct((b, t, hh * dv), BF16),
        grid=(b, hh, t // c),
        in_specs=[pl.BlockSpec((1, c, dk), lambda bi, h, n: (bi, n, h)),
                  pl.BlockSpec((1, c, dk), lambda bi, h, n: (bi, n, nq + h)),
                  pl.BlockSpec((1, c, dv), lambda bi, h, n: (bi, n, nv + h)),
                  pl.BlockSpec((1, c, dv), lambda bi, h, n: (bi, n, nv + hh + h)),
                  pl.BlockSpec((c, dk), lambda bi, h, n: (n, 0)),
                  pl.BlockSpec((c, dk), lambda bi, h, n: (n, 0)),
                  pl.BlockSpec((dk, dk), lambda bi, h, n: (0, 0)),
                  pl.BlockSpec((1, c, c), lambda bi, h, n: (h, 0, 0)),
                  pl.BlockSpec((1, c, 1), lambda bi, h, n: (h, 0, 0)),
                  pl.BlockSpec((1, c, 1), lambda bi, h, n: (h, 0, 0)),
                  pl.BlockSpec((1, SUBLANES, LANES), lambda bi, h, n: (h, 0, 0))],
        out_specs=pl.BlockSpec((1, c, dv), lambda bi, h, n: (bi, n, h)),
        scratch_shapes=[pltpu.VMEM((dk, dv), F32)],
        compiler_params=_cparams("parallel", "parallel", "arbitrary"),
        name=name,
    )(qkvg, qkvg, qkvg, qkvg, jnp.sin(theta), jnp.cos(theta), rot, dm, qd, kd, cd)


def _pad_axis(a, axis, to):
    pad = [(0, 0)] * a.ndim
    pad[axis] = (0, to - a.shape[axis])
    return jnp.pad(a, pad)


def kernel(x, mem, mix_norm_g, xattn_norm_g, mem_norm_g, xattn_wq, xattn_wkv, xattn_wo, ffn_norm_g, ffn_w_gate_up, ffn_w_down, conv_w_in, conv_b_in, conv_dw, conv_dw_b, conv_ln_g, conv_ln_b, conv_w_out, conv_b_out, rwkv_mix, rwkv_w_rkv, rwkv_w0, rwkv_w1, rwkv_w2, rwkv_a0, rwkv_a1, rwkv_a2, rwkv_g1, rwkv_g2, rwkv_k_k, rwkv_k_a, rwkv_r_k, rwkv_gn_g, rwkv_gn_b, rwkv_w_o, sgu_w_in, sgu_b_in, sgu_ln_g, sgu_ln_b, sgu_ws, sgu_bs, sgu_w_out, ret_w_in, ret_w_out, final_norm_g):
    b, t, d = x.shape
    n_mem = mem.shape[1]
    depth = mix_norm_g.shape[0]
    m = b * t
    mem2d = mem.reshape(b * n_mem, d)

    for i in range(depth):
        mixer, j = i % 4, i // 4
        g = mix_norm_g[i]
        if mixer == 0:
            z = _norm_mm(x.reshape(m, d), g, conv_w_in[j].astype(BF16), conv_b_in[j], glu=True,
                         name="conv_in")
            x = _conv_tail(z.reshape(b, t, d), conv_dw[j], conv_dw_b[j], conv_ln_g[j], conv_ln_b[j],
                           conv_w_out[j].astype(BF16), conv_b_out[j], x, name="conv_tail")
        elif mixer == 1:
            lw = _pad_axis(rwkv_w1[j], 1, LANES)
            la = _pad_axis(rwkv_a1[j], 1, LANES)
            lg = _pad_axis(rwkv_g1[j], 1, 2 * LANES)
            p = dict(mix=rwkv_mix[j].astype(F32),
                     wr=rwkv_w_rkv[j, 0].astype(BF16), wk=rwkv_w_rkv[j, 1].astype(BF16),
                     wv=rwkv_w_rkv[j, 2].astype(BF16),
                     w0=_row2(rwkv_w0[j]), w1=lw.astype(BF16),
                     w2=_pad_axis(rwkv_w2[j], 0, LANES).astype(BF16),
                     a0=_row2(rwkv_a0[j]), a1=la.astype(BF16),
                     a2=_pad_axis(rwkv_a2[j], 0, LANES).astype(BF16),
                     g1=lg.astype(BF16), g2=_pad_axis(rwkv_g2[j], 0, 2 * LANES).astype(BF16),
                     k_k=_row2(rwkv_k_k[j]), k_a=_row2(rwkv_k_a[j]), r_k=_row2(rwkv_r_k[j]),
                     gn_g=rwkv_gn_g[j], gn_b=rwkv_gn_b[j], w_o=rwkv_w_o[j].astype(BF16))
            x = _rwkv_mixer(x, g, p)
        elif mixer == 2:
            uv = _norm_mm(x.reshape(m, d), g, sgu_w_in[j].astype(BF16), sgu_b_in[j], act="gelu",
                          name="sgu_in")
            x = _sgu_tail(uv.reshape(b, t, -1), sgu_ln_g[j], sgu_ln_b[j], sgu_ws[j], sgu_bs[j],
                          sgu_w_out[j].astype(BF16), x, name="sgu_tail")
        else:
            qkvg = _norm_mm(x.reshape(m, d), g, ret_w_in[j].astype(BF16), name="ret_in")
            o = _ret_core(qkvg.reshape(b, t, -1), t, d, name="ret_core")
            x = _mm_res(o.reshape(m, -1), ret_w_out[j].astype(BF16), x.reshape(m, d),
                        name="ret_out").reshape(b, t, d)

        kv = _norm_mm(mem2d, mem_norm_g[i], xattn_wkv[i].astype(BF16), name="xattn_kv")
        x = _xattn(x, xattn_norm_g[i], xattn_wq[i].astype(BF16), kv.reshape(b, n_mem, 2 * d),
                   xattn_wo[i].astype(BF16), name="xattn")
        fg = final_norm_g if i == depth - 1 else None
        x = _swiglu(x.reshape(m, d), ffn_norm_g[i], ffn_w_gate_up[i].astype(BF16),
                    ffn_w_down[i].astype(BF16), fg, name="swiglu").reshape(b, t, d)
    return x
```

```python
import functools
import math

import jax
import jax.numpy as jnp
from jax import lax
from jax.experimental import pallas as pl
from jax.experimental.pallas import tpu as pltpu

F32 = jnp.float32
BF16 = jnp.bfloat16

NORM_EPS = 1e-6
LN_EPS = 1e-5

XA_HEADS = 4
CONV_WIDTH = 31
CONV_HALO = 32
RWKV_HEAD_DIM = 64
RWKV_GN_EPS = RWKV_HEAD_DIM * 1e-5
SGU_CHUNK = 128
SGU_GROUPS = 8
RET_HEADS = 4
RET_CHUNK = 512

LANES = 128
MXU_WIDTH = 256
SUBLANES = 8
VMEM_LIMIT_BYTES = 56 * 1024 * 1024


def _cparams(*sem):
    return pltpu.CompilerParams(dimension_semantics=sem, vmem_limit_bytes=VMEM_LIMIT_BYTES)


def _rms(x, g):
    return x * lax.rsqrt(jnp.mean(x * x, axis=-1, keepdims=True) + NORM_EPS) * g


def _layernorm(x, g, b, eps):
    mu = jnp.mean(x, axis=-1, keepdims=True)
    xc = x - mu
    var = jnp.mean(xc * xc, axis=-1, keepdims=True)
    return xc * lax.rsqrt(var + eps) * g + b


def _dot(a, b):
    return jnp.dot(a, b, preferred_element_type=F32)


def _dot_nt(a, b):
    return lax.dot_general(a, b, (((1,), (1,)), ((), ())), preferred_element_type=F32)


def _head_sum(x, e):
    return _dot(x.astype(BF16), e)


def _row2(v):
    return v.reshape(1, -1).astype(F32)


def _norm_mm_kernel(*refs, act, glu, has_bias, chunks):
    if has_bias:
        x_ref, g_ref, w_ref, b_ref, o_ref = refs
    else:
        x_ref, g_ref, w_ref, o_ref = refs
    xn = _rms(x_ref[...], g_ref[...]).astype(BF16)
    n_out = o_ref.shape[1]
    for lo, hi in chunks:
        y = _dot(xn, w_ref[:, lo:hi])
        if has_bias:
            y = y + b_ref[:, lo:hi]
        if glu:
            y2 = _dot(xn, w_ref[:, n_out + lo:n_out + hi])
            if has_bias:
                y2 = y2 + b_ref[:, n_out + lo:n_out + hi]
            y = y * jax.nn.sigmoid(y2)
        elif act == "gelu":
            y = 0.5 * y * (1.0 + lax.erf(y * math.sqrt(0.5)))
        o_ref[:, lo:hi] = y.astype(o_ref.dtype)


def _norm_mm(x2d, g, w, bias=None, *, act=None, glu=False, out_dtype=BF16, tm=1024, chunk=4 * MXU_WIDTH, name):
    m, k = x2d.shape
    n_w = w.shape[1]
    n_out = n_w // 2 if glu else n_w
    assert m % tm == 0 and n_out % MXU_WIDTH == 0
    chunks = tuple((lo, min(lo + chunk, n_out)) for lo in range(0, n_out, chunk))
    has_bias = bias is not None
    resident = dict(pipeline_mode=pl.Buffered(1))
    args = [x2d, _row2(g), w]
    specs = [pl.BlockSpec((tm, k), lambda i: (i, 0)),
             pl.BlockSpec((1, k), lambda i: (0, 0)),
             pl.BlockSpec((k, n_w), lambda i: (0, 0), **resident)]
    if has_bias:
        args.append(_row2(bias))
        specs.append(pl.BlockSpec((1, n_w), lambda i: (0, 0)))
    return pl.pallas_call(
        functools.partial(_norm_mm_kernel, act=act, glu=glu, has_bias=has_bias, chunks=chunks),
        out_shape=jax.ShapeDtypeStruct((m, n_out), out_dtype),
        grid=(m // tm,),
        in_specs=specs,
        out_specs=pl.BlockSpec((tm, n_out), lambda i: (i, 0)),
        compiler_params=_cparams("parallel"),
        name=name,
    )(*args)


def _mm_res_kernel(a_ref, w_ref, r_ref, o_ref):
    o_ref[...] = r_ref[...] + _dot(a_ref[...], w_ref[...])


def _mm_res(a, w, res, *, tm=1024, name):
    m, k = a.shape
    n = w.shape[1]
    assert m % tm == 0
    return pl.pallas_call(
        _mm_res_kernel,
        out_shape=jax.ShapeDtypeStruct((m, n), F32),
        grid=(m // tm,),
        in_specs=[pl.BlockSpec((tm, k), lambda i: (i, 0)),
                  pl.BlockSpec((k, n), lambda i: (0, 0)),
                  pl.BlockSpec((tm, n), lambda i: (i, 0))],
        out_specs=pl.BlockSpec((tm, n), lambda i: (i, 0)),
        compiler_params=_cparams("parallel"),
        name=name,
    )(a, w, res)


def _xattn_kernel(x_ref, g_ref, wq_ref, kv_ref, wo_ref, o_ref, oh_ref, *, heads):
    x = x_ref[0]
    d = x.shape[-1]
    hd = d // heads
    xn = _rms(x, g_ref[...]).astype(BF16)
    q = _dot(xn, wq_ref[...]).astype(BF16)
    scores = [_dot_nt(q[:, h * hd:(h + 1) * hd], kv_ref[0, :, h * hd:(h + 1) * hd]) * (hd ** -0.5)
              for h in range(heads)]
    probs = []
    for s in scores:
        p = jnp.exp(s - jnp.max(s, axis=-1, keepdims=True))
        probs.append((p / jnp.sum(p, axis=-1, keepdims=True)).astype(BF16))
    for h in range(heads):
        oh_ref[:, h * hd:(h + 1) * hd] = _dot(probs[h], kv_ref[0, :, d + h * hd:d + (h + 1) * hd]).astype(BF16)
    o_ref[0] = x + _dot(oh_ref[...], wo_ref[...])


def _xattn(x, g, wq, kv, wo, *, tm=1024, name):
    b, t, d = x.shape
    m = kv.shape[1]
    assert t % tm == 0
    return pl.pallas_call(
        functools.partial(_xattn_kernel, heads=XA_HEADS),
        out_shape=jax.ShapeDtypeStruct((b, t, d), F32),
        grid=(b, t // tm),
        in_specs=[pl.BlockSpec((1, tm, d), lambda bi, i: (bi, i, 0)),
                  pl.BlockSpec((1, d), lambda bi, i: (0, 0)),
                  pl.BlockSpec((d, d), lambda bi, i: (0, 0)),
                  pl.BlockSpec((1, m, 2 * d), lambda bi, i: (bi, 0, 0)),
                  pl.BlockSpec((d, d), lambda bi, i: (0, 0))],
        out_specs=pl.BlockSpec((1, tm, d), lambda bi, i: (bi, i, 0)),
        scratch_shapes=[pltpu.VMEM((tm, d), BF16)],
        compiler_params=_cparams("parallel", "parallel"),
        name=name,
    )(x, _row2(g), wq, kv, wo)


def _swiglu_kernel(*refs, final, chunks):
    if final:
        x_ref, g_ref, wg_ref, wu_ref, wd_ref, fg_ref, o_ref = refs
    else:
        x_ref, g_ref, wg_ref, wu_ref, wd_ref, o_ref = refs
    x = x_ref[...]
    xn = _rms(x, g_ref[...]).astype(BF16)
    y = x
    for lo, hi in chunks:
        gt = _dot(xn, wg_ref[:, lo:hi])
        up = _dot(xn, wu_ref[:, lo:hi])
        h = (gt * jax.nn.sigmoid(gt) * up).astype(BF16)
        y = y + _dot(h, wd_ref[lo:hi, :])
    if final:
        y = _rms(y, fg_ref[...])
    o_ref[...] = y


def _swiglu(x2d, g, w_gate_up, w_down, final_g=None, *, tm=1024, chunk=3 * MXU_WIDTH, name):
    m, d = x2d.shape
    ff = w_down.shape[0]
    assert ff % MXU_WIDTH == 0 and m % tm == 0
    chunks = tuple((lo, min(lo + chunk, ff)) for lo in range(0, ff, chunk))
    final = final_g is not None
    resident = dict(pipeline_mode=pl.Buffered(1))
    args = [x2d, _row2(g), w_gate_up, w_gate_up, w_down]
    specs = [pl.BlockSpec((tm, d), lambda i: (i, 0)),
             pl.BlockSpec((1, d), lambda i: (0, 0)),
             pl.BlockSpec((d, ff), lambda i: (0, 0), **resident),
             pl.BlockSpec((d, ff), lambda i: (0, 1), **resident),
             pl.BlockSpec((ff, d), lambda i: (0, 0), **resident)]
    if final:
        args.append(_row2(final_g))
        specs.append(pl.BlockSpec((1, d), lambda i: (0, 0)))
    return pl.pallas_call(
        functools.partial(_swiglu_kernel, final=final, chunks=chunks),
        out_shape=jax.ShapeDtypeStruct((m, d), F32),
        grid=(m // tm,),
        in_specs=specs,
        out_specs=pl.BlockSpec((tm, d), lambda i: (i, 0)),
        compiler_params=_cparams("parallel"),
        name=name,
    )(*args)


def _conv_tail_kernel(z_ref, zh_ref, dw_ref, dwb_ref, lg_ref, lb_ref, w_ref, b_ref, x_ref,
                      o_ref, zz_ref, sh_ref, c_ref, dwx_ref, *, rows):
    tm, d = z_ref.shape[1], z_ref.shape[2]
    not_first = (pl.program_id(1) != 0).astype(F32)
    zz_ref[0:CONV_HALO, :] = zh_ref[0].astype(F32) * not_first
    zz_ref[CONV_HALO:, :] = z_ref[0].astype(F32)
    for s in range(1, SUBLANES):
        sh_ref[s - 1] = zz_ref[pl.ds(s, sh_ref.shape[1]), :]
    for k in range(CONV_WIDTH):
        dwx_ref[k] = jnp.broadcast_to(dw_ref[k:k + 1, :], (SUBLANES, d))
    dwx_ref[CONV_WIDTH] = jnp.broadcast_to(dwb_ref[...], (SUBLANES, d))
    base = CONV_HALO - (CONV_WIDTH - 1)
    groups = rows // SUBLANES

    def chunk(c, carry):
        r0 = pl.multiple_of(c * rows, rows)
        acc = [dwx_ref[CONV_WIDTH]] * groups
        for k in range(CONV_WIDTH):
            blk, s = divmod(base + k, SUBLANES)
            src = zz_ref if s == 0 else sh_ref.at[s - 1]
            wk = dwx_ref[k]
            for gi in range(groups):
                acc[gi] = acc[gi] + src[pl.ds(r0 + (blk + gi) * SUBLANES, SUBLANES), :] * wk
        for gi in range(groups):
            c_ref[pl.ds(r0 + gi * SUBLANES, SUBLANES), :] = acc[gi]
        return carry

    lax.fori_loop(0, tm // rows, chunk, 0)
    y = _layernorm(c_ref[...], lg_ref[...], lb_ref[...], LN_EPS)
    act = (y * jax.nn.sigmoid(y)).astype(BF16)
    o_ref[0] = x_ref[0] + _dot(act, w_ref[...]) + b_ref[...]


def _conv_tail(z, dw, dw_b, ln_g, ln_b, w_out, b_out, x, *, tm=512, rows=32, name):
    b, t, d = x.shape
    assert t % tm == 0 and tm % rows == 0 and rows % SUBLANES == 0
    hb = tm // CONV_HALO
    vec = pl.BlockSpec((1, d), lambda bi, i: (0, 0))
    return pl.pallas_call(
        functools.partial(_conv_tail_kernel, rows=rows),
        out_shape=jax.ShapeDtypeStruct((b, t, d), F32),
        grid=(b, t // tm),
        in_specs=[pl.BlockSpec((1, tm, d), lambda bi, i: (bi, i, 0)),
                  pl.BlockSpec((1, CONV_HALO, d), lambda bi, i: (bi, jnp.maximum(i * hb - 1, 0), 0)),
                  pl.BlockSpec((CONV_WIDTH, d), lambda bi, i: (0, 0)),
                  vec, vec, vec,
                  pl.BlockSpec((d, d), lambda bi, i: (0, 0)),
                  vec,
                  pl.BlockSpec((1, tm, d), lambda bi, i: (bi, i, 0))],
        out_specs=pl.BlockSpec((1, tm, d), lambda bi, i: (bi, i, 0)),
        scratch_shapes=[pltpu.VMEM((tm + CONV_HALO, d), F32),
                        pltpu.VMEM((SUBLANES - 1, tm + CONV_HALO - SUBLANES, d), F32),
                        pltpu.VMEM((tm, d), F32),
                        pltpu.VMEM((CONV_WIDTH + 1, SUBLANES, d), F32)],
        compiler_params=_cparams("parallel", "parallel"),
        name=name,
    )(z, z, dw.astype(F32), _row2(dw_b), _row2(ln_g), _row2(ln_b), w_out, _row2(b_out), x)


def _softplus(x):
    return jnp.maximum(x, 0.0) + jnp.log(1.0 + jnp.exp(-jnp.abs(x)))


def _rwkv_prep_kernel(x_ref, xh_ref, g_ref, mix_ref, wr_ref, wk_ref, wv_ref, w0_ref, w1_ref, w2_ref,
                      a0_ref, a1_ref, a2_ref, g1_ref, g2_ref, kk_ref, ka_ref, rk_ref, e_ref, e2_ref, et_ref,
                      e_o, an_o, b_o, k_o, wr_o, v_o, bv_o, g_o, s_o, hs_ref):
    tm = x_ref.shape[1]
    gn = g_ref[...]
    hn = _rms(x_ref[0], gn)
    not_first = (pl.program_id(1) != 0).astype(F32)
    hs_ref[0:SUBLANES, :] = _rms(xh_ref[0], gn) * not_first
    hs_ref[SUBLANES:, :] = hn
    delta = hs_ref[pl.ds(SUBLANES - 1, tm), :] - hn

    def mixed(c):
        return (hn + delta * mix_ref[c:c + 1, :]).astype(BF16)

    r = _dot(mixed(0), wr_ref[...])
    k = _dot(mixed(2), wk_ref[...])
    v = _dot(mixed(3), wv_ref[...])
    w_lora = _dot(jnp.tanh(_dot(mixed(1), w1_ref[...])).astype(BF16), w2_ref[...])
    w = -_softplus(-(w0_ref[...] + w_lora)) - 0.5
    e_neg_log = jnp.exp(w).astype(BF16)
    decay = jnp.exp(-e_neg_log.astype(F32))
    a = jax.nn.sigmoid(a0_ref[...] + _dot(_dot(mixed(4), a1_ref[...]).astype(BF16), a2_ref[...]))
    gate = _dot(jax.nn.sigmoid(_dot(mixed(5), g1_ref[...])).astype(BF16), g2_ref[...])

    e, e2, et = e_ref[...], e2_ref[...], et_ref[...]
    kk = k * kk_ref[...]
    ss = _dot(_head_sum(kk * kk, e).astype(BF16), et)
    kk = kk / jnp.maximum(jnp.sqrt(ss), 1e-12)
    k = k * (1.0 + (a - 1.0) * ka_ref[...])
    b = kk * a
    bonus = _dot(_head_sum(r * k * rk_ref[...], e).astype(BF16), et)

    e_o[0] = e_neg_log
    an_o[0] = (-kk).astype(an_o.dtype)
    b_o[0] = b.astype(b_o.dtype)
    k_o[0] = k.astype(k_o.dtype)
    wr_o[0] = (decay * r).astype(wr_o.dtype)
    v_o[0] = v.astype(v_o.dtype)
    bv_o[0] = (bonus * v).astype(bv_o.dtype)
    g_o[0] = gate.astype(g_o.dtype)
    s_o[0] = _head_sum(b * r, e) + _head_sum(k * r, e2)


def _rwkv_prep(x, g, p, e, e2, et, *, tm=256, name):
    b, t, d = x.shape
    assert t % tm == 0
    hb = tm // SUBLANES
    tile = pl.BlockSpec((1, tm, d), lambda bi, i: (bi, i, 0))
    vec = pl.BlockSpec((1, d), lambda bi, i: (0, 0))

    def full(a):
        return pl.BlockSpec(a.shape, lambda bi, i: (0,) * a.ndim)

    consts = [p["mix"], p["wr"], p["wk"], p["wv"], p["w0"], p["w1"], p["w2"], p["a0"], p["a1"], p["a2"],
              p["g1"], p["g2"], p["k_k"], p["k_a"], p["r_k"], e, e2, et]
    bf16_tile = jax.ShapeDtypeStruct((b, t, d), BF16)
    return pl.pallas_call(
        _rwkv_prep_kernel,
        out_shape=[bf16_tile] * 8 + [jax.ShapeDtypeStruct((b, t, LANES), F32)],
        grid=(b, t // tm),
        in_specs=[tile,
                  pl.BlockSpec((1, SUBLANES, d), lambda bi, i: (bi, jnp.maximum(i * hb - 1, 0), 0)),
                  vec] + [full(a) for a in consts],
        out_specs=[tile] * 8 + [pl.BlockSpec((1, tm, LANES), lambda bi, i: (bi, i, 0))],
        scratch_shapes=[pltpu.VMEM((tm + SUBLANES, d), F32)],
        compiler_params=_cparams("parallel", "parallel"),
        name=name,
    )(x, x, _row2(g), *consts)


def _rwkv_scan_kernel(e_ref, b_ref, k_ref, v_ref, sc_ref, an_ref, wr_ref, ann_ref, wrn_ref,
                      y_ref, s_ref, sa_ref, ys_ref, bs_ref, ks_ref, as_ref, ws_ref, pe_ref):
    tb, nk, nh = e_ref.shape
    nv = s_ref.shape[1]

    @pl.when(pl.program_id(0) == 0)
    def _():
        s_ref[...] = jnp.zeros_like(s_ref)
        sa_ref[...] = jnp.zeros_like(sa_ref)
        ys_ref[...] = jnp.zeros_like(ys_ref)

    cum = jnp.zeros((nk, nh), F32)
    for t in range(tb):
        cum = cum + e_ref[t].astype(F32)
        grow, shrink = jnp.exp(cum), jnp.exp(-cum)
        bs_ref[t] = b_ref[t].astype(F32) * grow
        ks_ref[t] = k_ref[t].astype(F32) * grow
        a_next = an_ref[t + 1] if t + 1 < tb else ann_ref[0]
        w_next = wr_ref[t + 1] if t + 1 < tb else wrn_ref[0]
        as_ref[t] = a_next.astype(F32) * shrink
        ws_ref[t] = w_next.astype(F32) * shrink
    pe_ref[...] = jnp.exp(-cum)

    def step(t, ln, last):
        sa, ys, v = sa_ref[:, ln], ys_ref[:, ln], v_ref[t, :, ln].astype(F32)
        y = ys + sa * sc_ref[t, 0:1, ln] + v * sc_ref[t, 1:2, ln]
        yc = y - jnp.sum(y, axis=0, keepdims=True) * (1.0 / nv)
        var = jnp.sum(yc * yc, axis=0, keepdims=True) * (1.0 / nv)
        y_ref[t, :, ln] = (yc * lax.rsqrt(var + RWKV_GN_EPS)).astype(y_ref.dtype)
        sa_n = jnp.zeros_like(sa)
        ys_n = jnp.zeros_like(sa)
        for j in range(nk):
            s = s_ref[j, :, ln] + sa * bs_ref[t, j:j + 1, ln] + v * ks_ref[t, j:j + 1, ln]
            s_ref[j, :, ln] = s * pe_ref[j:j + 1, ln] if last else s
            sa_n = sa_n + s * as_ref[t, j:j + 1, ln]
            ys_n = ys_n + s * ws_ref[t, j:j + 1, ln]
        sa_ref[:, ln] = sa_n
        ys_ref[:, ln] = ys_n

    for hb in range(nh // LANES):
        ln = slice(hb * LANES, (hb + 1) * LANES)

        def body(t, carry, ln=ln):
            step(t, ln, False)
            return carry

        lax.fori_loop(0, tb - 1, body, 0)
        step(tb - 1, ln, True)


def _rwkv_scan(e, an, bb, kk, wr, v, sc, *, tb=32, name):
    t, n, nh = e.shape
    assert t % tb == 0 and nh % LANES == 0
    blk = pl.BlockSpec((tb, n, nh), lambda i: (i, 0, 0))
    row = pl.BlockSpec((tb, 2, nh), lambda i: (i, 0, 0))
    nxt = pl.BlockSpec((1, n, nh), lambda i: (jnp.minimum((i + 1) * tb, t - 1), 0, 0))
    step_rows = pltpu.VMEM((tb, n, nh), F32)
    return pl.pallas_call(
        _rwkv_scan_kernel,
        out_shape=jax.ShapeDtypeStruct((t, n, nh), BF16),
        grid=(t // tb,),
        in_specs=[blk, blk, blk, blk, row, blk, blk, nxt, nxt],
        out_specs=blk,
        scratch_shapes=[pltpu.VMEM((n, n, nh), F32), pltpu.VMEM((n, nh), F32), pltpu.VMEM((n, nh), F32),
                        step_rows, step_rows, step_rows, step_rows, pltpu.VMEM((n, nh), F32)],
        compiler_params=_cparams("arbitrary"),
        name=name,
    )(e, bb, kk, v, sc, an, wr, an, wr)


def _rwkv_post_kernel(y_ref, bv_ref, g_ref, x_ref, gg_ref, gb_ref, w_ref, o_ref):
    z = ((y_ref[...].astype(F32) * gg_ref[...] + gb_ref[...] + bv_ref[...].astype(F32))
         * g_ref[...].astype(F32))
    o_ref[...] = x_ref[...] + _dot(z.astype(BF16), w_ref[...])


def _rwkv_post(y, bv, g, x2d, gn_g, gn_b, w_o, *, tm=1024, name):
    m, d = x2d.shape
    tile = pl.BlockSpec((tm, d), lambda i: (i, 0))
    vec = pl.BlockSpec((1, d), lambda i: (0, 0))
    return pl.pallas_call(
        _rwkv_post_kernel,
        out_shape=jax.ShapeDtypeStruct((m, d), F32),
        grid=(m // tm,),
        in_specs=[tile, tile, tile, tile, vec, vec, pl.BlockSpec((d, d), lambda i: (0, 0))],
        out_specs=tile,
        compiler_params=_cparams("parallel"),
        name=name,
    )(y, bv, g, x2d, _row2(gn_g), _row2(gn_b), w_o)


def _rwkv_mixer(x, g, p):
    b, t, d = x.shape
    n = RWKV_HEAD_DIM
    h = d // n
    lane = jnp.arange(LANES)[None, :]
    head = (jnp.arange(d) // n)[:, None]
    e = (head == lane).astype(BF16)
    e2 = (head + h == lane).astype(BF16)
    et = e.T
    el, an, bb, kk, wr, v, bv, gate, s = _rwkv_prep(x, g, p, e, e2, et, name="rwkv_prep")

    def tm(a):
        return a.reshape(b, t, h, n).transpose(1, 3, 0, 2).reshape(t, n, b * h)

    sc = s[:, :, :2 * h].reshape(b, t, 2, h).transpose(1, 2, 0, 3).reshape(t, 2, b * h)
    y = _rwkv_scan(tm(el), tm(an), tm(bb), tm(kk), tm(wr), tm(v), sc, name="rwkv_scan")
    y = y.reshape(t, n, b, h).transpose(2, 0, 3, 1).reshape(b * t, d)
    out = _rwkv_post(y, bv.reshape(b * t, d), gate.reshape(b * t, d), x.reshape(b * t, d),
                     p["gn_g"], p["gn_b"], p["w_o"], name="rwkv_post")
    return out.reshape(b, t, d)


def _sgu_tail_kernel(u_ref, v_ref, lg_ref, lb_ref, ws_ref, bst_ref, w_ref, x_ref, o_ref, vn_ref, gt_ref):
    tm, width = u_ref.shape[1], u_ref.shape[2]
    c = SGU_CHUNK
    gd = width // SGU_GROUPS
    vn_ref[...] = _layernorm(v_ref[0].astype(F32), lg_ref[...], lb_ref[...], LN_EPS).astype(BF16)
    tri = lax.broadcasted_iota(jnp.int32, (c, c), 0) >= lax.broadcasted_iota(jnp.int32, (c, c), 1)
    for gi in range(SGU_GROUPS):
        wsm = jnp.where(tri, ws_ref[gi], 0.0).astype(BF16)
        bias = bst_ref[:, gi:gi + 1]
        cols = slice(gi * gd, (gi + 1) * gd)
        for ci in range(tm // c):
            rows = slice(ci * c, (ci + 1) * c)
            mixed = _dot(wsm, vn_ref[rows, cols]) + bias
            gt_ref[rows, cols] = (u_ref[0, rows, cols].astype(F32) * mixed).astype(BF16)
    o_ref[0] = x_ref[0] + _dot(gt_ref[...], w_ref[...])


def _sgu_tail(uv, ln_g, ln_b, ws, bs, w_out, x, *, tm=512, name):
    b, t, d = x.shape
    width = uv.shape[-1] // 2
    assert t % tm == 0 and tm % SGU_CHUNK == 0
    return pl.pallas_call(
        _sgu_tail_kernel,
        out_shape=jax.ShapeDtypeStruct((b, t, d), F32),
        grid=(b, t // tm),
        in_specs=[pl.BlockSpec((1, tm, width), lambda bi, i: (bi, i, 0)),
                  pl.BlockSpec((1, tm, width), lambda bi, i: (bi, i, 1)),
                  pl.BlockSpec((1, width), lambda bi, i: (0, 0)),
                  pl.BlockSpec((1, width), lambda bi, i: (0, 0)),
                  pl.BlockSpec(ws.shape, lambda bi, i: (0, 0, 0)),
                  pl.BlockSpec((SGU_CHUNK, SGU_GROUPS), lambda bi, i: (0, 0)),
                  pl.BlockSpec((width, d), lambda bi, i: (0, 0)),
                  pl.BlockSpec((1, tm, d), lambda bi, i: (bi, i, 0))],
        out_specs=pl.BlockSpec((1, tm, d), lambda bi, i: (bi, i, 0)),
        scratch_shapes=[pltpu.VMEM((tm, width), BF16), pltpu.VMEM((tm, width), BF16)],
        compiler_params=_cparams("parallel", "parallel"),
        name=name,
    )(uv, uv, _row2(ln_g), _row2(ln_b), ws.astype(F32), bs.T.astype(F32), w_out, x)


def _ret_core_kernel(q_ref, k_ref, v_ref, g_ref, sin_ref, cos_ref, rot_ref, dm_ref, qd_ref, kd_ref, cd_ref,
                     o_ref, r_ref):
    @pl.when(pl.program_id(2) == 0)
    def _():
        r_ref[...] = jnp.zeros_like(r_ref)

    sin, cos, rot = sin_ref[...], cos_ref[...], rot_ref[...]
    qb, kb, vb = q_ref[0], k_ref[0], v_ref[0]
    q = qb.astype(F32) * cos + _dot(qb, rot) * sin
    k = (kb.astype(F32) * cos + _dot(kb, rot) * sin) * (qb.shape[-1] ** -0.5)
    half = qb.shape[0] // 2
    q16, k16 = q.astype(BF16), k.astype(BF16)
    s_ee = (_dot_nt(q16[:half], k16[:half]) * dm_ref[0, :half, :half]).astype(BF16)
    s_le = (_dot_nt(q16[half:], k16[:half]) * dm_ref[0, half:, :half]).astype(BF16)
    s_ll = (_dot_nt(q16[half:], k16[half:]) * dm_ref[0, half:, half:]).astype(BF16)
    inner = jnp.concatenate([_dot(s_ee, vb[:half]), _dot(s_le, vb[:half]) + _dot(s_ll, vb[half:])], axis=0)
    r = r_ref[...]
    cross = _dot((q * qd_ref[0]).astype(BF16), r.astype(BF16))
    ks = (k * kd_ref[0]).astype(BF16)
    kv = lax.dot_general(ks, vb, (((0,), (0,)), ((), ())), preferred_element_type=F32)
    r_ref[...] = r * cd_ref[0, 0:1, 0:1] + kv
    o = inner + cross
    o = o * lax.rsqrt(jnp.mean(o * o, axis=-1, keepdims=True) + NORM_EPS)
    gate = g_ref[0].astype(F32)
    o_ref[0] = (gate * jax.nn.sigmoid(gate) * o).astype(o_ref.dtype)


def _ret_core(qkvg, t, d, *, name):
    b = qkvg.shape[0]
    hh, c = RET_HEADS, RET_CHUNK
    assert t % c == 0
    dk = d // hh
    dv = 2 * dk
    angle = jnp.repeat(1.0 / (10000.0 ** jnp.linspace(0.0, 1.0, dk // 2, dtype=F32)), 2)
    theta = jnp.arange(t, dtype=F32)[:, None] * angle[None]
    log_gamma = jnp.log(1.0 - 2.0 ** (-5.0 - jnp.arange(hh, dtype=F32)))
    idx = jnp.arange(c, dtype=F32)
    diff = idx[:, None] - idx[None, :]
    dm = jnp.where(diff[None] >= 0, jnp.exp(log_gamma[:, None, None] * jnp.maximum(diff, 0.0)[None]), 0.0)
    qd = jnp.exp(log_gamma[:, None] * (idx + 1.0))[:, :, None]
    kd = jnp.exp(log_gamma[:, None] * (c - 1.0 - idx))[:, :, None]
    cd = jnp.broadcast_to(jnp.exp(log_gamma * c)[:, None, None], (hh, SUBLANES, LANES))
    src, dst = jnp.arange(dk)[:, None], jnp.arange(dk)[None, :]
    rot = (jnp.where((dst % 2 == 1) & (src == dst - 1), 1.0, 0.0)
           - jnp.where((dst % 2 == 0) & (src == dst + 1), 1.0, 0.0)).astype(BF16)
    nq = d // dk
    nv = 2 * d // dv
    return pl.pallas_call(
        _ret_core_kernel,
        out_shape=jax.ShapeDtypeStruct((b, t, hh * dv), BF16),
        grid=(b, hh, t // c),
        in_specs=[pl.BlockSpec((1, c, dk), lambda bi, h, n: (bi, n, h)),
                  pl.BlockSpec((1, c, dk), lambda bi, h, n: (bi, n, nq + h)),
                  pl.BlockSpec((1, c, dv), lambda bi, h, n: (bi, n, nv + h)),
                  pl.BlockSpec((1, c, dv), lambda bi, h, n: (bi, n, nv + hh + h)),
                  pl.BlockSpec((c, dk), lambda bi, h, n: (n, 0)),
                  pl.BlockSpec((c, dk), lambda bi, h, n: (n, 0)),
                  pl.BlockSpec((dk, dk), lambda bi, h, n: (0, 0)),
                  pl.BlockSpec((1, c, c), lambda bi, h, n: (h, 0, 0)),
                  pl.BlockSpec((1, c, 1), lambda bi, h, n: (h, 0, 0)),
                  pl.BlockSpec((1, c, 1), lambda bi, h, n: (h, 0, 0)),
                  pl.BlockSpec((1, SUBLANES, LANES), lambda bi, h, n: (h, 0, 0))],
        out_specs=pl.BlockSpec((1, c, dv), lambda bi, h, n: (bi, n, h)),
        scratch_shapes=[pltpu.VMEM((dk, dv), F32)],
        compiler_params=_cparams("parallel", "parallel", "arbitrary"),
        name=name,
    )(qkvg, qkvg, qkvg, qkvg, jnp.sin(theta), jnp.cos(theta), rot, dm, qd, kd, cd)


def _pad_axis(a, axis, to):
    pad = [(0, 0)] * a.ndim
    pad[axis] = (0, to - a.shape[axis])
    return jnp.pad(a, pad)


def kernel(x, mem, mix_norm_g, xattn_norm_g, mem_norm_g, xattn_wq, xattn_wkv, xattn_wo, ffn_norm_g, ffn_w_gate_up, ffn_w_down, conv_w_in, conv_b_in, conv_dw, conv_dw_b, conv_ln_g, conv_ln_b, conv_w_out, conv_b_out, rwkv_mix, rwkv_w_rkv, rwkv_w0, rwkv_w1, rwkv_w2, rwkv_a0, rwkv_a1, rwkv_a2, rwkv_g1, rwkv_g2, rwkv_k_k, rwkv_k_a, rwkv_r_k, rwkv_gn_g, rwkv_gn_b, rwkv_w_o, sgu_w_in, sgu_b_in, sgu_ln_g, sgu_ln_b, sgu_ws, sgu_bs, sgu_w_out, ret_w_in, ret_w_out, final_norm_g):
    b, t, d = x.shape
    n_mem = mem.shape[1]
    depth = mix_norm_g.shape[0]
    m = b * t
    mem2d = mem.reshape(b * n_mem, d)

    for i in range(depth):
        mixer, j = i % 4, i // 4
        g = mix_norm_g[i]
        if mixer == 0:
            z = _norm_mm(x.reshape(m, d), g, conv_w_in[j].astype(BF16), conv_b_in[j], glu=True,
                         name="conv_in")
            x = _conv_tail(z.reshape(b, t, d), conv_dw[j], conv_dw_b[j], conv_ln_g[j], conv_ln_b[j],
                           conv_w_out[j].astype(BF16), conv_b_out[j], x, name="conv_tail")
        elif mixer == 1:
            lw = _pad_axis(rwkv_w1[j], 1, LANES)
            la = _pad_axis(rwkv_a1[j], 1, LANES)
            lg = _pad_axis(rwkv_g1[j], 1, 2 * LANES)
            p = dict(mix=rwkv_mix[j].astype(F32),
                     wr=rwkv_w_rkv[j, 0].astype(BF16), wk=rwkv_w_rkv[j, 1].astype(BF16),
                     wv=rwkv_w_rkv[j, 2].astype(BF16),
                     w0=_row2(rwkv_w0[j]), w1=lw.astype(BF16),
                     w2=_pad_axis(rwkv_w2[j], 0, LANES).astype(BF16),
                     a0=_row2(rwkv_a0[j]), a1=la.astype(BF16),
                     a2=_pad_axis(rwkv_a2[j], 0, LANES).astype(BF16),
                     g1=lg.astype(BF16), g2=_pad_axis(rwkv_g2[j], 0, 2 * LANES).astype(BF16),
                     k_k=_row2(rwkv_k_k[j]), k_a=_row2(rwkv_k_a[j]), r_k=_row2(rwkv_r_k[j]),
                     gn_g=rwkv_gn_g[j], gn_b=rwkv_gn_b[j], w_o=rwkv_w_o[j].astype(BF16))
            x = _rwkv_mixer(x, g, p)
        elif mixer == 2:
            uv = _norm_mm(x.reshape(m, d), g, sgu_w_in[j].astype(BF16), sgu_b_in[j], act="gelu",
                          name="sgu_in")
            x = _sgu_tail(uv.reshape(b, t, -1), sgu_ln_g[j], sgu_ln_b[j], sgu_ws[j], sgu_bs[j],
                          sgu_w_out[j].astype(BF16), x, name="sgu_tail")
        else:
            qkvg = _norm_mm(x.reshape(m, d), g, ret_w_in[j].astype(BF16), name="ret_in")
            o = _ret_core(qkvg.reshape(b, t, -1), t, d, name="ret_core")
            x = _mm_res(o.reshape(m, -1), ret_w_out[j].astype(BF16), x.reshape(m, d),
                        name="ret_out").reshape(b, t, d)

        kv = _norm_mm(mem2d, mem_norm_g[i], xattn_wkv[i].astype(BF16), name="xattn_kv")
        x = _xattn(x, xattn_norm_g[i], xattn_wq[i].astype(BF16), kv.reshape(b, n_mem, 2 * d),
                   xattn_wo[i].astype(BF16), name="xattn")
        fg = final_norm_g if i == depth - 1 else None
        x = _swiglu(x.reshape(m, d), ffn_norm_g[i], ffn_w_gate_up[i].astype(BF16),
                    ffn_w_down[i].astype(BF16), fg, name="swiglu").reshape(b, t, d)
    return x
```

```python
import functools
import math

import jax
import jax.numpy as jnp
from jax import lax
from jax.experimental import pallas as pl
from jax.experimental.pallas import tpu as pltpu

F32 = jnp.float32
BF16 = jnp.bfloat16

NORM_EPS = 1e-6
LN_EPS = 1e-5

XA_HEADS = 4
CONV_WIDTH = 31
CONV_HALO = 32
RWKV_HEAD_DIM = 64
RWKV_GN_EPS = RWKV_HEAD_DIM * 1e-5
SGU_CHUNK = 128
SGU_GROUPS = 8
RET_HEADS = 4
RET_CHUNK = 512
RET_HEADS_PER_STEP = 2

LANES = 128
MXU_WIDTH = 256
SUBLANES = 8
VMEM_LIMIT_BYTES = 56 * 1024 * 1024


def _cparams(*sem):
    return pltpu.CompilerParams(dimension_semantics=sem, vmem_limit_bytes=VMEM_LIMIT_BYTES)


def _rms(x, g):
    return x * lax.rsqrt(jnp.mean(x * x, axis=-1, keepdims=True) + NORM_EPS) * g


def _layernorm(x, g, b, eps):
    mu = jnp.mean(x, axis=-1, keepdims=True)
    xc = x - mu
    var = jnp.mean(xc * xc, axis=-1, keepdims=True)
    return xc * lax.rsqrt(var + eps) * g + b


def _dot(a, b):
    return jnp.dot(a, b, preferred_element_type=F32)


def _dot_nt(a, b):
    return lax.dot_general(a, b, (((1,), (1,)), ((), ())), preferred_element_type=F32)


def _head_sum(x, e):
    return _dot(x.astype(BF16), e)


def _row2(v):
    return v.reshape(1, -1).astype(F32)


def _norm_mm_kernel(*refs, act, glu, has_bias, chunks):
    if has_bias:
        x_ref, g_ref, w_ref, b_ref, o_ref = refs
    else:
        x_ref, g_ref, w_ref, o_ref = refs
    xn = _rms(x_ref[...], g_ref[...]).astype(BF16)
    n_out = o_ref.shape[1]
    for lo, hi in chunks:
        y = _dot(xn, w_ref[:, lo:hi])
        if has_bias:
            y = y + b_ref[:, lo:hi]
        if glu:
            y2 = _dot(xn, w_ref[:, n_out + lo:n_out + hi])
            if has_bias:
                y2 = y2 + b_ref[:, n_out + lo:n_out + hi]
            y = y * jax.nn.sigmoid(y2)
        elif act == "gelu":
            y = 0.5 * y * (1.0 + lax.erf(y * math.sqrt(0.5)))
        o_ref[:, lo:hi] = y.astype(o_ref.dtype)


def _norm_mm(x2d, g, w, bias=None, *, act=None, glu=False, out_dtype=BF16, tm=1024, chunk=4 * MXU_WIDTH, name):
    m, k = x2d.shape
    n_w = w.shape[1]
    n_out = n_w // 2 if glu else n_w
    assert m % tm == 0 and n_out % MXU_WIDTH == 0
    chunks = tuple((lo, min(lo + chunk, n_out)) for lo in range(0, n_out, chunk))
    has_bias = bias is not None
    resident = dict(pipeline_mode=pl.Buffered(1))
    args = [x2d, _row2(g), w]
    specs = [pl.BlockSpec((tm, k), lambda i: (i, 0)),
             pl.BlockSpec((1, k), lambda i: (0, 0)),
             pl.BlockSpec((k, n_w), lambda i: (0, 0), **resident)]
    if has_bias:
        args.append(_row2(bias))
        specs.append(pl.BlockSpec((1, n_w), lambda i: (0, 0)))
    return pl.pallas_call(
        functools.partial(_norm_mm_kernel, act=act, glu=glu, has_bias=has_bias, chunks=chunks),
        out_shape=jax.ShapeDtypeStruct((m, n_out), out_dtype),
        grid=(m // tm,),
        in_specs=specs,
        out_specs=pl.BlockSpec((tm, n_out), lambda i: (i, 0)),
        compiler_params=_cparams("parallel"),
        name=name,
    )(*args)


def _mm_res_kernel(a_ref, w_ref, r_ref, o_ref):
    o_ref[...] = r_ref[...] + _dot(a_ref[...], w_ref[...])


def _mm_res(a, w, res, *, tm=1024, name):
    m, k = a.shape
    n = w.shape[1]
    assert m % tm == 0
    return pl.pallas_call(
        _mm_res_kernel,
        out_shape=jax.ShapeDtypeStruct((m, n), F32),
        grid=(m // tm,),
        in_specs=[pl.BlockSpec((tm, k), lambda i: (i, 0)),
                  pl.BlockSpec((k, n), lambda i: (0, 0)),
                  pl.BlockSpec((tm, n), lambda i: (i, 0))],
        out_specs=pl.BlockSpec((tm, n), lambda i: (i, 0)),
        compiler_params=_cparams("parallel"),
        name=name,
    )(a, w, res)


def _xattn_kernel(x_ref, g_ref, wq_ref, kv_ref, wo_ref, o_ref, oh_ref, *, heads):
    x = x_ref[0]
    d = x.shape[-1]
    hd = d // heads
    xn = _rms(x, g_ref[...]).astype(BF16)
    q = _dot(xn, wq_ref[...]).astype(BF16)
    scores = [_dot_nt(q[:, h * hd:(h + 1) * hd], kv_ref[0, :, h * hd:(h + 1) * hd]) * (hd ** -0.5)
              for h in range(heads)]
    probs = []
    for s in scores:
        p = jnp.exp(s - jnp.max(s, axis=-1, keepdims=True))
        probs.append((p / jnp.sum(p, axis=-1, keepdims=True)).astype(BF16))
    for h in range(heads):
        oh_ref[:, h * hd:(h + 1) * hd] = _dot(probs[h], kv_ref[0, :, d + h * hd:d + (h + 1) * hd]).astype(BF16)
    o_ref[0] = x + _dot(oh_ref[...], wo_ref[...])


def _xattn(x, g, wq, kv, wo, *, tm=1024, name):
    b, t, d = x.shape
    m = kv.shape[1]
    assert t % tm == 0
    return pl.pallas_call(
        functools.partial(_xattn_kernel, heads=XA_HEADS),
        out_shape=jax.ShapeDtypeStruct((b, t, d), F32),
        grid=(b, t // tm),
        in_specs=[pl.BlockSpec((1, tm, d), lambda bi, i: (bi, i, 0)),
                  pl.BlockSpec((1, d), lambda bi, i: (0, 0)),
                  pl.BlockSpec((d, d), lambda bi, i: (0, 0)),
                  pl.BlockSpec((1, m, 2 * d), lambda bi, i: (bi, 0, 0)),
                  pl.BlockSpec((d, d), lambda bi, i: (0, 0))],
        out_specs=pl.BlockSpec((1, tm, d), lambda bi, i: (bi, i, 0)),
        scratch_shapes=[pltpu.VMEM((tm, d), BF16)],
        compiler_params=_cparams("parallel", "parallel"),
        name=name,
    )(x, _row2(g), wq, kv, wo)


def _swiglu_kernel(*refs, final, chunks):
    if final:
        x_ref, g_ref, wg_ref, wu_ref, wd_ref, fg_ref, o_ref = refs
    else:
        x_ref, g_ref, wg_ref, wu_ref, wd_ref, o_ref = refs
    x = x_ref[...]
    xn = _rms(x, g_ref[...]).astype(BF16)
    y = x
    for lo, hi in chunks:
        gt = _dot(xn, wg_ref[:, lo:hi])
        up = _dot(xn, wu_ref[:, lo:hi])
        h = (gt * jax.nn.sigmoid(gt) * up).astype(BF16)
        y = y + _dot(h, wd_ref[lo:hi, :])
    if final:
        y = _rms(y, fg_ref[...])
    o_ref[...] = y


def _swiglu(x2d, g, w_gate_up, w_down, final_g=None, *, tm=1024, chunk=3 * MXU_WIDTH, name):
    m, d = x2d.shape
    ff = w_down.shape[0]
    assert ff % MXU_WIDTH == 0 and m % tm == 0
    chunks = tuple((lo, min(lo + chunk, ff)) for lo in range(0, ff, chunk))
    final = final_g is not None
    resident = dict(pipeline_mode=pl.Buffered(1))
    args = [x2d, _row2(g), w_gate_up, w_gate_up, w_down]
    specs = [pl.BlockSpec((tm, d), lambda i: (i, 0)),
             pl.BlockSpec((1, d), lambda i: (0, 0)),
             pl.BlockSpec((d, ff), lambda i: (0, 0), **resident),
             pl.BlockSpec((d, ff), lambda i: (0, 1), **resident),
             pl.BlockSpec((ff, d), lambda i: (0, 0), **resident)]
    if final:
        args.append(_row2(final_g))
        specs.append(pl.BlockSpec((1, d), lambda i: (0, 0)))
    return pl.pallas_call(
        functools.partial(_swiglu_kernel, final=final, chunks=chunks),
        out_shape=jax.ShapeDtypeStruct((m, d), F32),
        grid=(m // tm,),
        in_specs=specs,
        out_specs=pl.BlockSpec((tm, d), lambda i: (i, 0)),
        compiler_params=_cparams("parallel"),
        name=name,
    )(*args)


def _conv_tail_kernel(z_ref, zh_ref, dw_ref, dwb_ref, lg_ref, lb_ref, w_ref, b_ref, x_ref,
                      o_ref, zz_ref, sh_ref, c_ref, dwx_ref, *, rows):
    tm, d = z_ref.shape[1], z_ref.shape[2]
    not_first = (pl.program_id(1) != 0).astype(F32)
    zz_ref[0:CONV_HALO, :] = zh_ref[0].astype(F32) * not_first
    zz_ref[CONV_HALO:, :] = z_ref[0].astype(F32)
    for s in range(1, SUBLANES):
        sh_ref[s - 1] = zz_ref[pl.ds(s, sh_ref.shape[1]), :]
    for k in range(CONV_WIDTH):
        dwx_ref[k] = jnp.broadcast_to(dw_ref[k:k + 1, :], (SUBLANES, d))
    dwx_ref[CONV_WIDTH] = jnp.broadcast_to(dwb_ref[...], (SUBLANES, d))
    base = CONV_HALO - (CONV_WIDTH - 1)
    groups = rows // SUBLANES

    def chunk(c, carry):
        r0 = pl.multiple_of(c * rows, rows)
        acc = [dwx_ref[CONV_WIDTH]] * groups
        for k in range(CONV_WIDTH):
            blk, s = divmod(base + k, SUBLANES)
            src = zz_ref if s == 0 else sh_ref.at[s - 1]
            wk = dwx_ref[k]
            for gi in range(groups):
                acc[gi] = acc[gi] + src[pl.ds(r0 + (blk + gi) * SUBLANES, SUBLANES), :] * wk
        for gi in range(groups):
            c_ref[pl.ds(r0 + gi * SUBLANES, SUBLANES), :] = acc[gi]
        return carry

    lax.fori_loop(0, tm // rows, chunk, 0)
    y = _layernorm(c_ref[...], lg_ref[...], lb_ref[...], LN_EPS)
    act = (y * jax.nn.sigmoid(y)).astype(BF16)
    o_ref[0] = x_ref[0] + _dot(act, w_ref[...]) + b_ref[...]


def _conv_tail(z, dw, dw_b, ln_g, ln_b, w_out, b_out, x, *, tm=512, rows=32, name):
    b, t, d = x.shape
    assert t % tm == 0 and tm % rows == 0 and rows % SUBLANES == 0
    hb = tm // CONV_HALO
    vec = pl.BlockSpec((1, d), lambda bi, i: (0, 0))
    return pl.pallas_call(
        functools.partial(_conv_tail_kernel, rows=rows),
        out_shape=jax.ShapeDtypeStruct((b, t, d), F32),
        grid=(b, t // tm),
        in_specs=[pl.BlockSpec((1, tm, d), lambda bi, i: (bi, i, 0)),
                  pl.BlockSpec((1, CONV_HALO, d), lambda bi, i: (bi, jnp.maximum(i * hb - 1, 0), 0)),
                  pl.BlockSpec((CONV_WIDTH, d), lambda bi, i: (0, 0)),
                  vec, vec, vec,
                  pl.BlockSpec((d, d), lambda bi, i: (0, 0)),
                  vec,
                  pl.BlockSpec((1, tm, d), lambda bi, i: (bi, i, 0))],
        out_specs=pl.BlockSpec((1, tm, d), lambda bi, i: (bi, i, 0)),
        scratch_shapes=[pltpu.VMEM((tm + CONV_HALO, d), F32),
                        pltpu.VMEM((SUBLANES - 1, tm + CONV_HALO - SUBLANES, d), F32),
                        pltpu.VMEM((tm, d), F32),
                        pltpu.VMEM((CONV_WIDTH + 1, SUBLANES, d), F32)],
        compiler_params=_cparams("parallel", "parallel"),
        name=name,
    )(z, z, dw.astype(F32), _row2(dw_b), _row2(ln_g), _row2(ln_b), w_out, _row2(b_out), x)


def _softplus(x):
    return jnp.maximum(x, 0.0) + jnp.log(1.0 + jnp.exp(-jnp.abs(x)))


def _rwkv_prep_kernel(x_ref, xh_ref, g_ref, mix_ref, wr_ref, wk_ref, wv_ref, w0_ref, w1_ref, w2_ref,
                      a0_ref, a1_ref, a2_ref, g1_ref, g2_ref, kk_ref, ka_ref, rk_ref, e_ref, e2_ref, et_ref,
                      e_o, an_o, b_o, k_o, wr_o, v_o, bv_o, g_o, s_o, hs_ref):
    tm = x_ref.shape[1]
    gn = g_ref[...]
    hn = _rms(x_ref[0], gn)
    not_first = (pl.program_id(1) != 0).astype(F32)
    hs_ref[0:SUBLANES, :] = _rms(xh_ref[0], gn) * not_first
    hs_ref[SUBLANES:, :] = hn
    delta = hs_ref[pl.ds(SUBLANES - 1, tm), :] - hn

    def mixed(c):
        return (hn + delta * mix_ref[c:c + 1, :]).astype(BF16)

    r = _dot(mixed(0), wr_ref[...])
    k = _dot(mixed(2), wk_ref[...])
    v = _dot(mixed(3), wv_ref[...])
    w_lora = _dot(jnp.tanh(_dot(mixed(1), w1_ref[...])).astype(BF16), w2_ref[...])
    w = -_softplus(-(w0_ref[...] + w_lora)) - 0.5
    e_neg_log = jnp.exp(w).astype(BF16)
    decay = jnp.exp(-e_neg_log.astype(F32))
    a = jax.nn.sigmoid(a0_ref[...] + _dot(_dot(mixed(4), a1_ref[...]).astype(BF16), a2_ref[...]))
    gate = _dot(jax.nn.sigmoid(_dot(mixed(5), g1_ref[...])).astype(BF16), g2_ref[...])

    e, e2, et = e_ref[...], e2_ref[...], et_ref[...]
    kk = k * kk_ref[...]
    ss = _dot(_head_sum(kk * kk, e).astype(BF16), et)
    kk = kk / jnp.maximum(jnp.sqrt(ss), 1e-12)
    k = k * (1.0 + (a - 1.0) * ka_ref[...])
    b = kk * a
    bonus = _dot(_head_sum(r * k * rk_ref[...], e).astype(BF16), et)

    e_o[0] = e_neg_log
    an_o[0] = (-kk).astype(an_o.dtype)
    b_o[0] = b.astype(b_o.dtype)
    k_o[0] = k.astype(k_o.dtype)
    wr_o[0] = (decay * r).astype(wr_o.dtype)
    v_o[0] = v.astype(v_o.dtype)
    bv_o[0] = (bonus * v).astype(bv_o.dtype)
    g_o[0] = gate.astype(g_o.dtype)
    s_o[0] = _head_sum(b * r, e) + _head_sum(k * r, e2)


def _rwkv_prep(x, g, p, e, e2, et, *, tm=256, name):
    b, t, d = x.shape
    assert t % tm == 0
    hb = tm // SUBLANES
    tile = pl.BlockSpec((1, tm, d), lambda bi, i: (bi, i, 0))
    vec = pl.BlockSpec((1, d), lambda bi, i: (0, 0))

    def full(a):
        return pl.BlockSpec(a.shape, lambda bi, i: (0,) * a.ndim)

    consts = [p["mix"], p["wr"], p["wk"], p["wv"], p["w0"], p["w1"], p["w2"], p["a0"], p["a1"], p["a2"],
              p["g1"], p["g2"], p["k_k"], p["k_a"], p["r_k"], e, e2, et]
    bf16_tile = jax.ShapeDtypeStruct((b, t, d), BF16)
    return pl.pallas_call(
        _rwkv_prep_kernel,
        out_shape=[bf16_tile] * 8 + [jax.ShapeDtypeStruct((b, t, LANES), F32)],
        grid=(b, t // tm),
        in_specs=[tile,
                  pl.BlockSpec((1, SUBLANES, d), lambda bi, i: (bi, jnp.maximum(i * hb - 1, 0), 0)),
                  vec] + [full(a) for a in consts],
        out_specs=[tile] * 8 + [pl.BlockSpec((1, tm, LANES), lambda bi, i: (bi, i, 0))],
        scratch_shapes=[pltpu.VMEM((tm + SUBLANES, d), F32)],
        compiler_params=_cparams("parallel", "parallel"),
        name=name,
    )(x, x, _row2(g), *consts)


def _rwkv_scan_kernel(e_ref, b_ref, k_ref, v_ref, sc_ref, an_ref, wr_ref, ann_ref, wrn_ref,
                      y_ref, s_ref, sa_ref, ys_ref, bs_ref, ks_ref, as_ref, ws_ref, pe_ref):
    tb, nk, nh = e_ref.shape
    nv = s_ref.shape[1]

    @pl.when(pl.program_id(0) == 0)
    def _():
        s_ref[...] = jnp.zeros_like(s_ref)
        sa_ref[...] = jnp.zeros_like(sa_ref)
        ys_ref[...] = jnp.zeros_like(ys_ref)

    cum = jnp.zeros((nk, nh), F32)
    for t in range(tb):
        cum = cum + e_ref[t].astype(F32)
        grow, shrink = jnp.exp(cum), jnp.exp(-cum)
        bs_ref[t] = b_ref[t].astype(F32) * grow
        ks_ref[t] = k_ref[t].astype(F32) * grow
        a_next = an_ref[t + 1] if t + 1 < tb else ann_ref[0]
        w_next = wr_ref[t + 1] if t + 1 < tb else wrn_ref[0]
        as_ref[t] = a_next.astype(F32) * shrink
        ws_ref[t] = w_next.astype(F32) * shrink
    pe_ref[...] = jnp.exp(-cum)

    def step(t, ln, last):
        sa, ys, v = sa_ref[:, ln], ys_ref[:, ln], v_ref[t, :, ln].astype(F32)
        y = ys + sa * sc_ref[t, 0:1, ln] + v * sc_ref[t, 1:2, ln]
        yc = y - jnp.sum(y, axis=0, keepdims=True) * (1.0 / nv)
        var = jnp.sum(yc * yc, axis=0, keepdims=True) * (1.0 / nv)
        y_ref[t, :, ln] = (yc * lax.rsqrt(var + RWKV_GN_EPS)).astype(y_ref.dtype)
        sa_n = jnp.zeros_like(sa)
        ys_n = jnp.zeros_like(sa)
        for j in range(nk):
            s = s_ref[j, :, ln] + sa * bs_ref[t, j:j + 1, ln] + v * ks_ref[t, j:j + 1, ln]
            s_ref[j, :, ln] = s * pe_ref[j:j + 1, ln] if last else s
            sa_n = sa_n + s * as_ref[t, j:j + 1, ln]
            ys_n = ys_n + s * ws_ref[t, j:j + 1, ln]
        sa_ref[:, ln] = sa_n
        ys_ref[:, ln] = ys_n

    for hb in range(nh // LANES):
        ln = slice(hb * LANES, (hb + 1) * LANES)

        def body(t, carry, ln=ln):
            step(t, ln, False)
            return carry

        lax.fori_loop(0, tb - 1, body, 0)
        step(tb - 1, ln, True)


def _rwkv_scan(e, an, bb, kk, wr, v, sc, *, tb=32, name):
    t, n, nh = e.shape
    assert t % tb == 0 and nh % LANES == 0
    blk = pl.BlockSpec((tb, n, nh), lambda i: (i, 0, 0))
    row = pl.BlockSpec((tb, 2, nh), lambda i: (i, 0, 0))
    nxt = pl.BlockSpec((1, n, nh), lambda i: (jnp.minimum((i + 1) * tb, t - 1), 0, 0))
    step_rows = pltpu.VMEM((tb, n, nh), F32)
    return pl.pallas_call(
        _rwkv_scan_kernel,
        out_shape=jax.ShapeDtypeStruct((t, n, nh), BF16),
        grid=(t // tb,),
        in_specs=[blk, blk, blk, blk, row, blk, blk, nxt, nxt],
        out_specs=blk,
        scratch_shapes=[pltpu.VMEM((n, n, nh), F32), pltpu.VMEM((n, nh), F32), pltpu.VMEM((n, nh), F32),
                        step_rows, step_rows, step_rows, step_rows, pltpu.VMEM((n, nh), F32)],
        compiler_params=_cparams("arbitrary"),
        name=name,
    )(e, bb, kk, v, sc, an, wr, an, wr)


def _rwkv_post_kernel(y_ref, bv_ref, g_ref, x_ref, gg_ref, gb_ref, w_ref, o_ref):
    z = ((y_ref[...].astype(F32) * gg_ref[...] + gb_ref[...] + bv_ref[...].astype(F32))
         * g_ref[...].astype(F32))
    o_ref[...] = x_ref[...] + _dot(z.astype(BF16), w_ref[...])


def _rwkv_post(y, bv, g, x2d, gn_g, gn_b, w_o, *, tm=1024, name):
    m, d = x2d.shape
    tile = pl.BlockSpec((tm, d), lambda i: (i, 0))
    vec = pl.BlockSpec((1, d), lambda i: (0, 0))
    return pl.pallas_call(
        _rwkv_post_kernel,
        out_shape=jax.ShapeDtypeStruct((m, d), F32),
        grid=(m // tm,),
        in_specs=[tile, tile, tile, tile, vec, vec, pl.BlockSpec((d, d), lambda i: (0, 0))],
        out_specs=tile,
        compiler_params=_cparams("parallel"),
        name=name,
    )(y, bv, g, x2d, _row2(gn_g), _row2(gn_b), w_o)


def _rwkv_mixer(x, g, p):
    b, t, d = x.shape
    n = RWKV_HEAD_DIM
    h = d // n
    lane = jnp.arange(LANES)[None, :]
    head = (jnp.arange(d) // n)[:, None]
    e = (head == lane).astype(BF16)
    e2 = (head + h == lane).astype(BF16)
    et = e.T
    el, an, bb, kk, wr, v, bv, gate, s = _rwkv_prep(x, g, p, e, e2, et, name="rwkv_prep")

    def tm(a):
        return a.reshape(b, t, h, n).transpose(1, 3, 0, 2).reshape(t, n, b * h)

    sc = s[:, :, :2 * h].reshape(b, t, 2, h).transpose(1, 2, 0, 3).reshape(t, 2, b * h)
    y = _rwkv_scan(tm(el), tm(an), tm(bb), tm(kk), tm(wr), tm(v), sc, name="rwkv_scan")
    y = y.reshape(t, n, b, h).transpose(2, 0, 3, 1).reshape(b * t, d)
    out = _rwkv_post(y, bv.reshape(b * t, d), gate.reshape(b * t, d), x.reshape(b * t, d),
                     p["gn_g"], p["gn_b"], p["w_o"], name="rwkv_post")
    return out.reshape(b, t, d)


def _sgu_tail_kernel(u_ref, v_ref, lg_ref, lb_ref, ws_ref, bst_ref, w_ref, x_ref, o_ref, vn_ref, gt_ref):
    tm, width = u_ref.shape[1], u_ref.shape[2]
    c = SGU_CHUNK
    gd = width // SGU_GROUPS
    vn_ref[...] = _layernorm(v_ref[0].astype(F32), lg_ref[...], lb_ref[...], LN_EPS).astype(BF16)
    tri = lax.broadcasted_iota(jnp.int32, (c, c), 0) >= lax.broadcasted_iota(jnp.int32, (c, c), 1)
    for gi in range(SGU_GROUPS):
        wsm = jnp.where(tri, ws_ref[gi], 0.0).astype(BF16)
        bias = bst_ref[:, gi:gi + 1]
        cols = slice(gi * gd, (gi + 1) * gd)
        for ci in range(tm // c):
            rows = slice(ci * c, (ci + 1) * c)
            mixed = _dot(wsm, vn_ref[rows, cols]) + bias
            gt_ref[rows, cols] = (u_ref[0, rows, cols].astype(F32) * mixed).astype(BF16)
    o_ref[0] = x_ref[0] + _dot(gt_ref[...], w_ref[...])


def _sgu_tail(uv, ln_g, ln_b, ws, bs, w_out, x, *, tm=512, name):
    b, t, d = x.shape
    width = uv.shape[-1] // 2
    assert t % tm == 0 and tm % SGU_CHUNK == 0
    return pl.pallas_call(
        _sgu_tail_kernel,
        out_shape=jax.ShapeDtypeStruct((b, t, d), F32),
        grid=(b, t // tm),
        in_specs=[pl.BlockSpec((1, tm, width), lambda bi, i: (bi, i, 0)),
                  pl.BlockSpec((1, tm, width), lambda bi, i: (bi, i, 1)),
                  pl.BlockSpec((1, width), lambda bi, i: (0, 0)),
                  pl.BlockSpec((1, width), lambda bi, i: (0, 0)),
                  pl.BlockSpec(ws.shape, lambda bi, i: (0, 0, 0)),
                  pl.BlockSpec((SGU_CHUNK, SGU_GROUPS), lambda bi, i: (0, 0)),
                  pl.BlockSpec((width, d), lambda bi, i: (0, 0)),
                  pl.BlockSpec((1, tm, d), lambda bi, i: (bi, i, 0))],
        out_specs=pl.BlockSpec((1, tm, d), lambda bi, i: (bi, i, 0)),
        scratch_shapes=[pltpu.VMEM((tm, width), BF16), pltpu.VMEM((tm, width), BF16)],
        compiler_params=_cparams("parallel", "parallel"),
        name=name,
    )(uv, uv, _row2(ln_g), _row2(ln_b), ws.astype(F32), bs.T.astype(F32), w_out, x)


def _ret_core_kernel(q_ref, k_ref, v_ref, g_ref, sin_ref, cos_ref, rot_ref, dm_ref, qd_ref, kd_ref, cd_ref,
                     o_ref, r_ref):
    @pl.when(pl.program_id(2) == 0)
    def _():
        r_ref[...] = jnp.zeros_like(r_ref)

    sin, cos, rot = sin_ref[...], cos_ref[...], rot_ref[...]
    dk, dv = r_ref.shape[1], r_ref.shape[2]
    for hl in range(r_ref.shape[0]):
        qb = q_ref[0, :, hl * dk:(hl + 1) * dk]
        kb = k_ref[0, :, hl * dk:(hl + 1) * dk]
        vb = v_ref[0, :, hl * dv:(hl + 1) * dv]
        q = qb.astype(F32) * cos + _dot(qb, rot) * sin
        k = (kb.astype(F32) * cos + _dot(kb, rot) * sin) * (dk ** -0.5)
        half = qb.shape[0] // 2
        q16, k16 = q.astype(BF16), k.astype(BF16)
        s_ee = (_dot_nt(q16[:half], k16[:half]) * dm_ref[hl, :half, :half]).astype(BF16)
        s_le = (_dot_nt(q16[half:], k16[:half]) * dm_ref[hl, half:, :half]).astype(BF16)
        s_ll = (_dot_nt(q16[half:], k16[half:]) * dm_ref[hl, half:, half:]).astype(BF16)
        inner = jnp.concatenate([_dot(s_ee, vb[:half]), _dot(s_le, vb[:half]) + _dot(s_ll, vb[half:])], axis=0)
        r = r_ref[hl]
        cross = _dot((q * qd_ref[hl]).astype(BF16), r.astype(BF16))
        ks = (k * kd_ref[hl]).astype(BF16)
        kv = lax.dot_general(ks, vb, (((0,), (0,)), ((), ())), preferred_element_type=F32)
        r_ref[hl] = r * cd_ref[hl, 0:1, 0:1] + kv
        o = inner + cross
        o = o * lax.rsqrt(jnp.mean(o * o, axis=-1, keepdims=True) + NORM_EPS)
        gate = g_ref[0, :, hl * dv:(hl + 1) * dv].astype(F32)
        o_ref[0, :, hl * dv:(hl + 1) * dv] = (gate * jax.nn.sigmoid(gate) * o).astype(o_ref.dtype)


def _ret_core(qkvg, t, d, *, name):
    b = qkvg.shape[0]
    hh, c = RET_HEADS, RET_CHUNK
    assert t % c == 0
    dk = d // hh
    dv = 2 * dk
    angle = jnp.repeat(1.0 / (10000.0 ** jnp.linspace(0.0, 1.0, dk // 2, dtype=F32)), 2)
    theta = jnp.arange(t, dtype=F32)[:, None] * angle[None]
    log_gamma = jnp.log(1.0 - 2.0 ** (-5.0 - jnp.arange(hh, dtype=F32)))
    idx = jnp.arange(c, dtype=F32)
    diff = idx[:, None] - idx[None, :]
    dm = jnp.where(diff[None] >= 0, jnp.exp(log_gamma[:, None, None] * jnp.maximum(diff, 0.0)[None]), 0.0)
    qd = jnp.exp(log_gamma[:, None] * (idx + 1.0))[:, :, None]
    kd = jnp.exp(log_gamma[:, None] * (c - 1.0 - idx))[:, :, None]
    cd = jnp.broadcast_to(jnp.exp(log_gamma * c)[:, None, None], (hh, SUBLANES, LANES))
    src, dst = jnp.arange(dk)[:, None], jnp.arange(dk)[None, :]
    rot = (jnp.where((dst % 2 == 1) & (src == dst - 1), 1.0, 0.0)
           - jnp.where((dst % 2 == 0) & (src == dst + 1), 1.0, 0.0)).astype(BF16)
    hs = RET_HEADS_PER_STEP
    nq = d // (hs * dk)
    nv = 2 * d // (hs * dv)
    return pl.pallas_call(
        _ret_core_kernel,
        out_shape=jax.ShapeDtypeStruct((b, t, hh * dv), BF16),
        grid=(b, hh // hs, t // c),
        in_specs=[pl.BlockSpec((1, c, hs * dk), lambda bi, h, n: (bi, n, h)),
                  pl.BlockSpec((1, c, hs * dk), lambda bi, h, n: (bi, n, nq + h)),
                  pl.BlockSpec((1, c, hs * dv), lambda bi, h, n: (bi, n, nv + h)),
                  pl.BlockSpec((1, c, hs * dv), lambda bi, h, n: (bi, n, nv + hh // hs + h)),
                  pl.BlockSpec((c, dk), lambda bi, h, n: (n, 0)),
                  pl.BlockSpec((c, dk), lambda bi, h, n: (n, 0)),
                  pl.BlockSpec((dk, dk), lambda bi, h, n: (0, 0)),
                  pl.BlockSpec((hs, c, c), lambda bi, h, n: (h, 0, 0)),
                  pl.BlockSpec((hs, c, 1), lambda bi, h, n: (h, 0, 0)),
                  pl.BlockSpec((hs, c, 1), lambda bi, h, n: (h, 0, 0)),
                  pl.BlockSpec((hs, SUBLANES, LANES), lambda bi, h, n: (h, 0, 0))],
        out_specs=pl.BlockSpec((1, c, hs * dv), lambda bi, h, n: (bi, n, h)),
        scratch_shapes=[pltpu.VMEM((hs, dk, dv), F32)],
        compiler_params=_cparams("parallel", "parallel", "arbitrary"),
        name=name,
    )(qkvg, qkvg, qkvg, qkvg, jnp.sin(theta), jnp.cos(theta), rot, dm, qd, kd, cd)


def _pad_axis(a, axis, to):
    pad = [(0, 0)] * a.ndim
    pad[axis] = (0, to - a.shape[axis])
    return jnp.pad(a, pad)


def kernel(x, mem, mix_norm_g, xattn_norm_g, mem_norm_g, xattn_wq, xattn_wkv, xattn_wo, ffn_norm_g, ffn_w_gate_up, ffn_w_down, conv_w_in, conv_b_in, conv_dw, conv_dw_b, conv_ln_g, conv_ln_b, conv_w_out, conv_b_out, rwkv_mix, rwkv_w_rkv, rwkv_w0, rwkv_w1, rwkv_w2, rwkv_a0, rwkv_a1, rwkv_a2, rwkv_g1, rwkv_g2, rwkv_k_k, rwkv_k_a, rwkv_r_k, rwkv_gn_g, rwkv_gn_b, rwkv_w_o, sgu_w_in, sgu_b_in, sgu_ln_g, sgu_ln_b, sgu_ws, sgu_bs, sgu_w_out, ret_w_in, ret_w_out, final_norm_g):
    b, t, d = x.shape
    n_mem = mem.shape[1]
    depth = mix_norm_g.shape[0]
    m = b * t
    mem2d = mem.reshape(b * n_mem, d)

    for i in range(depth):
        mixer, j = i % 4, i // 4
        g = mix_norm_g[i]
        if mixer == 0:
            z = _norm_mm(x.reshape(m, d), g, conv_w_in[j].astype(BF16), conv_b_in[j], glu=True,
                         name="conv_in")
            x = _conv_tail(z.reshape(b, t, d), conv_dw[j], conv_dw_b[j], conv_ln_g[j], conv_ln_b[j],
                           conv_w_out[j].astype(BF16), conv_b_out[j], x, name="conv_tail")
        elif mixer == 1:
            lw = _pad_axis(rwkv_w1[j], 1, LANES)
            la = _pad_axis(rwkv_a1[j], 1, LANES)
            lg = _pad_axis(rwkv_g1[j], 1, 2 * LANES)
            p = dict(mix=rwkv_mix[j].astype(F32),
                     wr=rwkv_w_rkv[j, 0].astype(BF16), wk=rwkv_w_rkv[j, 1].astype(BF16),
                     wv=rwkv_w_rkv[j, 2].astype(BF16),
                     w0=_row2(rwkv_w0[j]), w1=lw.astype(BF16),
                     w2=_pad_axis(rwkv_w2[j], 0, LANES).astype(BF16),
                     a0=_row2(rwkv_a0[j]), a1=la.astype(BF16),
                     a2=_pad_axis(rwkv_a2[j], 0, LANES).astype(BF16),
                     g1=lg.astype(BF16), g2=_pad_axis(rwkv_g2[j], 0, 2 * LANES).astype(BF16),
                     k_k=_row2(rwkv_k_k[j]), k_a=_row2(rwkv_k_a[j]), r_k=_row2(rwkv_r_k[j]),
                     gn_g=rwkv_gn_g[j], gn_b=rwkv_gn_b[j], w_o=rwkv_w_o[j].astype(BF16))
            x = _rwkv_mixer(x, g, p)
        elif mixer == 2:
            uv = _norm_mm(x.reshape(m, d), g, sgu_w_in[j].astype(BF16), sgu_b_in[j], act="gelu",
                          name="sgu_in")
            x = _sgu_tail(uv.reshape(b, t, -1), sgu_ln_g[j], sgu_ln_b[j], sgu_ws[j], sgu_bs[j],
                          sgu_w_out[j].astype(BF16), x, name="sgu_tail")
        else:
            qkvg = _norm_mm(x.reshape(m, d), g, ret_w_in[j].astype(BF16), name="ret_in")
            o = _ret_core(qkvg.reshape(b, t, -1), t, d, name="ret_core")
            x = _mm_res(o.reshape(m, -1), ret_w_out[j].astype(BF16), x.reshape(m, d),
                        name="ret_out").reshape(b, t, d)

        kv = _norm_mm(mem2d, mem_norm_g[i], xattn_wkv[i].astype(BF16), name="xattn_kv")
        x = _xattn(x, xattn_norm_g[i], xattn_wq[i].astype(BF16), kv.reshape(b, n_mem, 2 * d),
                   xattn_wo[i].astype(BF16), name="xattn")
        fg = final_norm_g if i == depth - 1 else None
        x = _swiglu(x.reshape(m, d), ffn_norm_g[i], ffn_w_gate_up[i].astype(BF16),
                    ffn_w_down[i].astype(BF16), fg, name="swiglu").reshape(b, t, d)
    return x
```

```python
import functools
import math

import jax
import jax.numpy as jnp
from jax import lax
from jax.experimental import pallas as pl
from jax.experimental.pallas import tpu as pltpu

F32 = jnp.float32
BF16 = jnp.bfloat16

NORM_EPS = 1e-6
LN_EPS = 1e-5

XA_HEADS = 4
CONV_WIDTH = 31
CONV_HALO = 32
RWKV_HEAD_DIM = 64
RWKV_GN_EPS = RWKV_HEAD_DIM * 1e-5
SGU_CHUNK = 128
SGU_GROUPS = 8
RET_HEADS = 4
RET_CHUNK = 512
RET_HEADS_PER_STEP = 4

LANES = 128
MXU_WIDTH = 256
SUBLANES = 8
VMEM_LIMIT_BYTES = 56 * 1024 * 1024


def _cparams(*sem):
    return pltpu.CompilerParams(dimension_semantics=sem, vmem_limit_bytes=VMEM_LIMIT_BYTES)


def _rms(x, g):
    return x * lax.rsqrt(jnp.mean(x * x, axis=-1, keepdims=True) + NORM_EPS) * g


def _layernorm(x, g, b, eps):
    mu = jnp.mean(x, axis=-1, keepdims=True)
    xc = x - mu
    var = jnp.mean(xc * xc, axis=-1, keepdims=True)
    return xc * lax.rsqrt(var + eps) * g + b


def _dot(a, b):
    return jnp.dot(a, b, preferred_element_type=F32)


def _dot_nt(a, b):
    return lax.dot_general(a, b, (((1,), (1,)), ((), ())), preferred_element_type=F32)


def _head_sum(x, e):
    return _dot(x.astype(BF16), e)


def _row2(v):
    return v.reshape(1, -1).astype(F32)


def _norm_mm_kernel(*refs, act, glu, has_bias, chunks):
    if has_bias:
        x_ref, g_ref, w_ref, b_ref, o_ref = refs
    else:
        x_ref, g_ref, w_ref, o_ref = refs
    xn = _rms(x_ref[...], g_ref[...]).astype(BF16)
    n_out = o_ref.shape[1]
    for lo, hi in chunks:
        y = _dot(xn, w_ref[:, lo:hi])
        if has_bias:
            y = y + b_ref[:, lo:hi]
        if glu:
            y2 = _dot(xn, w_ref[:, n_out + lo:n_out + hi])
            if has_bias:
                y2 = y2 + b_ref[:, n_out + lo:n_out + hi]
            y = y * jax.nn.sigmoid(y2)
        elif act == "gelu":
            y = 0.5 * y * (1.0 + lax.erf(y * math.sqrt(0.5)))
        o_ref[:, lo:hi] = y.astype(o_ref.dtype)


def _norm_mm(x2d, g, w, bias=None, *, act=None, glu=False, out_dtype=BF16, tm=1024, chunk=4 * MXU_WIDTH, name):
    m, k = x2d.shape
    n_w = w.shape[1]
    n_out = n_w // 2 if glu else n_w
    assert m % tm == 0 and n_out % MXU_WIDTH == 0
    chunks = tuple((lo, min(lo + chunk, n_out)) for lo in range(0, n_out, chunk))
    has_bias = bias is not None
    resident = dict(pipeline_mode=pl.Buffered(1))
    args = [x2d, _row2(g), w]
    specs = [pl.BlockSpec((tm, k), lambda i: (i, 0)),
             pl.BlockSpec((1, k), lambda i: (0, 0)),
             pl.BlockSpec((k, n_w), lambda i: (0, 0), **resident)]
    if has_bias:
        args.append(_row2(bias))
        specs.append(pl.BlockSpec((1, n_w), lambda i: (0, 0)))
    return pl.pallas_call(
        functools.partial(_norm_mm_kernel, act=act, glu=glu, has_bias=has_bias, chunks=chunks),
        out_shape=jax.ShapeDtypeStruct((m, n_out), out_dtype),
        grid=(m // tm,),
        in_specs=specs,
        out_specs=pl.BlockSpec((tm, n_out), lambda i: (i, 0)),
        compiler_params=_cparams("parallel"),
        name=name,
    )(*args)


def _mm_res_kernel(a_ref, w_ref, r_ref, o_ref):
    o_ref[...] = r_ref[...] + _dot(a_ref[...], w_ref[...])


def _mm_res(a, w, res, *, tm=1024, name):
    m, k = a.shape
    n = w.shape[1]
    assert m % tm == 0
    return pl.pallas_call(
        _mm_res_kernel,
        out_shape=jax.ShapeDtypeStruct((m, n), F32),
        grid=(m // tm,),
        in_specs=[pl.BlockSpec((tm, k), lambda i: (i, 0)),
                  pl.BlockSpec((k, n), lambda i: (0, 0)),
                  pl.BlockSpec((tm, n), lambda i: (i, 0))],
        out_specs=pl.BlockSpec((tm, n), lambda i: (i, 0)),
        compiler_params=_cparams("parallel"),
        name=name,
    )(a, w, res)


def _xattn_kernel(x_ref, g_ref, wq_ref, kv_ref, wo_ref, o_ref, oh_ref, *, heads):
    x = x_ref[0]
    d = x.shape[-1]
    hd = d // heads
    xn = _rms(x, g_ref[...]).astype(BF16)
    q = _dot(xn, wq_ref[...]).astype(BF16)
    scores = [_dot_nt(q[:, h * hd:(h + 1) * hd], kv_ref[0, :, h * hd:(h + 1) * hd]) * (hd ** -0.5)
              for h in range(heads)]
    probs = []
    for s in scores:
        p = jnp.exp(s - jnp.max(s, axis=-1, keepdims=True))
        probs.append((p / jnp.sum(p, axis=-1, keepdims=True)).astype(BF16))
    for h in range(heads):
        oh_ref[:, h * hd:(h + 1) * hd] = _dot(probs[h], kv_ref[0, :, d + h * hd:d + (h + 1) * hd]).astype(BF16)
    o_ref[0] = x + _dot(oh_ref[...], wo_ref[...])


def _xattn(x, g, wq, kv, wo, *, tm=1024, name):
    b, t, d = x.shape
    m = kv.shape[1]
    assert t % tm == 0
    return pl.pallas_call(
        functools.partial(_xattn_kernel, heads=XA_HEADS),
        out_shape=jax.ShapeDtypeStruct((b, t, d), F32),
        grid=(b, t // tm),
        in_specs=[pl.BlockSpec((1, tm, d), lambda bi, i: (bi, i, 0)),
                  pl.BlockSpec((1, d), lambda bi, i: (0, 0)),
                  pl.BlockSpec((d, d), lambda bi, i: (0, 0), pipeline_mode=pl.Buffered(1)),
                  pl.BlockSpec((1, m, 2 * d), lambda bi, i: (bi, 0, 0)),
                  pl.BlockSpec((d, d), lambda bi, i: (0, 0), pipeline_mode=pl.Buffered(1))],
        out_specs=pl.BlockSpec((1, tm, d), lambda bi, i: (bi, i, 0)),
        scratch_shapes=[pltpu.VMEM((tm, d), BF16)],
        compiler_params=_cparams("parallel", "parallel"),
        name=name,
    )(x, _row2(g), wq, kv, wo)


def _swiglu_kernel(*refs, final, chunks):
    if final:
        x_ref, g_ref, wg_ref, wu_ref, wd_ref, fg_ref, o_ref = refs
    else:
        x_ref, g_ref, wg_ref, wu_ref, wd_ref, o_ref = refs
    x = x_ref[...]
    xn = _rms(x, g_ref[...]).astype(BF16)
    y = x
    for lo, hi in chunks:
        gt = _dot(xn, wg_ref[:, lo:hi])
        up = _dot(xn, wu_ref[:, lo:hi])
        h = (gt * jax.nn.sigmoid(gt) * up).astype(BF16)
        y = y + _dot(h, wd_ref[lo:hi, :])
    if final:
        y = _rms(y, fg_ref[...])
    o_ref[...] = y


def _swiglu(x2d, g, w_gate_up, w_down, final_g=None, *, tm=1024, chunk=3 * MXU_WIDTH, name):
    m, d = x2d.shape
    ff = w_down.shape[0]
    assert ff % MXU_WIDTH == 0 and m % tm == 0
    chunks = tuple((lo, min(lo + chunk, ff)) for lo in range(0, ff, chunk))
    final = final_g is not None
    resident = dict(pipeline_mode=pl.Buffered(1))
    args = [x2d, _row2(g), w_gate_up, w_gate_up, w_down]
    specs = [pl.BlockSpec((tm, d), lambda i: (i, 0)),
             pl.BlockSpec((1, d), lambda i: (0, 0)),
             pl.BlockSpec((d, ff), lambda i: (0, 0), **resident),
             pl.BlockSpec((d, ff), lambda i: (0, 1), **resident),
             pl.BlockSpec((ff, d), lambda i: (0, 0), **resident)]
    if final:
        args.append(_row2(final_g))
        specs.append(pl.BlockSpec((1, d), lambda i: (0, 0)))
    return pl.pallas_call(
        functools.partial(_swiglu_kernel, final=final, chunks=chunks),
        out_shape=jax.ShapeDtypeStruct((m, d), F32),
        grid=(m // tm,),
        in_specs=specs,
        out_specs=pl.BlockSpec((tm, d), lambda i: (i, 0)),
        compiler_params=_cparams("parallel"),
        name=name,
    )(*args)


def _conv_tail_kernel(z_ref, zh_ref, dw_ref, dwb_ref, lg_ref, lb_ref, w_ref, b_ref, x_ref,
                      o_ref, zz_ref, sh_ref, c_ref, dwx_ref, *, rows):
    tm, d = z_ref.shape[1], z_ref.shape[2]
    not_first = (pl.program_id(1) != 0).astype(F32)
    zz_ref[0:CONV_HALO, :] = zh_ref[0].astype(F32) * not_first
    zz_ref[CONV_HALO:, :] = z_ref[0].astype(F32)
    for s in range(1, SUBLANES):
        sh_ref[s - 1] = zz_ref[pl.ds(s, sh_ref.shape[1]), :]
    for k in range(CONV_WIDTH):
        dwx_ref[k] = jnp.broadcast_to(dw_ref[k:k + 1, :], (SUBLANES, d))
    dwx_ref[CONV_WIDTH] = jnp.broadcast_to(dwb_ref[...], (SUBLANES, d))
    base = CONV_HALO - (CONV_WIDTH - 1)
    groups = rows // SUBLANES

    def chunk(c, carry):
        r0 = pl.multiple_of(c * rows, rows)
        acc = [dwx_ref[CONV_WIDTH]] * groups
        for k in range(CONV_WIDTH):
            blk, s = divmod(base + k, SUBLANES)
            src = zz_ref if s == 0 else sh_ref.at[s - 1]
            wk = dwx_ref[k]
            for gi in range(groups):
                acc[gi] = acc[gi] + src[pl.ds(r0 + (blk + gi) * SUBLANES, SUBLANES), :] * wk
        for gi in range(groups):
            c_ref[pl.ds(r0 + gi * SUBLANES, SUBLANES), :] = acc[gi]
        return carry

    lax.fori_loop(0, tm // rows, chunk, 0)
    y = _layernorm(c_ref[...], lg_ref[...], lb_ref[...], LN_EPS)
    act = (y * jax.nn.sigmoid(y)).astype(BF16)
    o_ref[0] = x_ref[0] + _dot(act, w_ref[...]) + b_ref[...]


def _conv_tail(z, dw, dw_b, ln_g, ln_b, w_out, b_out, x, *, tm=512, rows=32, name):
    b, t, d = x.shape
    assert t % tm == 0 and tm % rows == 0 and rows % SUBLANES == 0
    hb = tm // CONV_HALO
    vec = pl.BlockSpec((1, d), lambda bi, i: (0, 0))
    return pl.pallas_call(
        functools.partial(_conv_tail_kernel, rows=rows),
        out_shape=jax.ShapeDtypeStruct((b, t, d), F32),
        grid=(b, t // tm),
        in_specs=[pl.BlockSpec((1, tm, d), lambda bi, i: (bi, i, 0)),
                  pl.BlockSpec((1, CONV_HALO, d), lambda bi, i: (bi, jnp.maximum(i * hb - 1, 0), 0)),
                  pl.BlockSpec((CONV_WIDTH, d), lambda bi, i: (0, 0)),
                  vec, vec, vec,
                  pl.BlockSpec((d, d), lambda bi, i: (0, 0)),
                  vec,
                  pl.BlockSpec((1, tm, d), lambda bi, i: (bi, i, 0))],
        out_specs=pl.BlockSpec((1, tm, d), lambda bi, i: (bi, i, 0)),
        scratch_shapes=[pltpu.VMEM((tm + CONV_HALO, d), F32),
                        pltpu.VMEM((SUBLANES - 1, tm + CONV_HALO - SUBLANES, d), F32),
                        pltpu.VMEM((tm, d), F32),
                        pltpu.VMEM((CONV_WIDTH + 1, SUBLANES, d), F32)],
        compiler_params=_cparams("parallel", "parallel"),
        name=name,
    )(z, z, dw.astype(F32), _row2(dw_b), _row2(ln_g), _row2(ln_b), w_out, _row2(b_out), x)


def _softplus(x):
    return jnp.maximum(x, 0.0) + jnp.log(1.0 + jnp.exp(-jnp.abs(x)))


def _rwkv_prep_kernel(x_ref, xh_ref, g_ref, mix_ref, wr_ref, wk_ref, wv_ref, w0_ref, w1_ref, w2_ref,
                      a0_ref, a1_ref, a2_ref, g1_ref, g2_ref, kk_ref, ka_ref, rk_ref, e_ref, e2_ref, et_ref,
                      e_o, an_o, b_o, k_o, wr_o, v_o, bv_o, g_o, s_o, hs_ref):
    tm = x_ref.shape[1]
    gn = g_ref[...]
    hn = _rms(x_ref[0], gn)
    not_first = (pl.program_id(1) != 0).astype(F32)
    hs_ref[0:SUBLANES, :] = _rms(xh_ref[0], gn) * not_first
    hs_ref[SUBLANES:, :] = hn
    delta = hs_ref[pl.ds(SUBLANES - 1, tm), :] - hn

    def mixed(c):
        return (hn + delta * mix_ref[c:c + 1, :]).astype(BF16)

    r = _dot(mixed(0), wr_ref[...])
    k = _dot(mixed(2), wk_ref[...])
    v = _dot(mixed(3), wv_ref[...])
    w_lora = _dot(jnp.tanh(_dot(mixed(1), w1_ref[...])).astype(BF16), w2_ref[...])
    w = -_softplus(-(w0_ref[...] + w_lora)) - 0.5
    e_neg_log = jnp.exp(w).astype(BF16)
    decay = jnp.exp(-e_neg_log.astype(F32))
    a = jax.nn.sigmoid(a0_ref[...] + _dot(_dot(mixed(4), a1_ref[...]).astype(BF16), a2_ref[...]))
    gate = _dot(jax.nn.sigmoid(_dot(mixed(5), g1_ref[...])).astype(BF16), g2_ref[...])

    e, e2, et = e_ref[...], e2_ref[...], et_ref[...]
    kk = k * kk_ref[...]
    ss = _dot(_head_sum(kk * kk, e).astype(BF16), et)
    kk = kk / jnp.maximum(jnp.sqrt(ss), 1e-12)
    k = k * (1.0 + (a - 1.0) * ka_ref[...])
    b = kk * a
    bonus = _dot(_head_sum(r * k * rk_ref[...], e).astype(BF16), et)

    e_o[0] = e_neg_log
    an_o[0] = (-kk).astype(an_o.dtype)
    b_o[0] = b.astype(b_o.dtype)
    k_o[0] = k.astype(k_o.dtype)
    wr_o[0] = (decay * r).astype(wr_o.dtype)
    v_o[0] = v.astype(v_o.dtype)
    bv_o[0] = (bonus * v).astype(bv_o.dtype)
    g_o[0] = gate.astype(g_o.dtype)
    s_o[0] = _head_sum(b * r, e) + _head_sum(k * r, e2)


def _rwkv_prep(x, g, p, e, e2, et, *, tm=256, name):
    b, t, d = x.shape
    assert t % tm == 0
    hb = tm // SUBLANES
    tile = pl.BlockSpec((1, tm, d), lambda bi, i: (bi, i, 0))
    vec = pl.BlockSpec((1, d), lambda bi, i: (0, 0))

    def full(a):
        return pl.BlockSpec(a.shape, lambda bi, i: (0,) * a.ndim)

    consts = [p["mix"], p["wr"], p["wk"], p["wv"], p["w0"], p["w1"], p["w2"], p["a0"], p["a1"], p["a2"],
              p["g1"], p["g2"], p["k_k"], p["k_a"], p["r_k"], e, e2, et]
    bf16_tile = jax.ShapeDtypeStruct((b, t, d), BF16)
    return pl.pallas_call(
        _rwkv_prep_kernel,
        out_shape=[bf16_tile] * 8 + [jax.ShapeDtypeStruct((b, t, LANES), F32)],
        grid=(b, t // tm),
        in_specs=[tile,
                  pl.BlockSpec((1, SUBLANES, d), lambda bi, i: (bi, jnp.maximum(i * hb - 1, 0), 0)),
                  vec] + [full(a) for a in consts],
        out_specs=[tile] * 8 + [pl.BlockSpec((1, tm, LANES), lambda bi, i: (bi, i, 0))],
        scratch_shapes=[pltpu.VMEM((tm + SUBLANES, d), F32)],
        compiler_params=_cparams("parallel", "parallel"),
        name=name,
    )(x, x, _row2(g), *consts)


def _rwkv_scan_kernel(e_ref, b_ref, k_ref, v_ref, sc_ref, an_ref, wr_ref, ann_ref, wrn_ref,
                      y_ref, s_ref, sa_ref, ys_ref, bs_ref, ks_ref, as_ref, ws_ref, pe_ref):
    tb, nk, nh = e_ref.shape
    nv = s_ref.shape[1]

    @pl.when(pl.program_id(0) == 0)
    def _():
        s_ref[...] = jnp.zeros_like(s_ref)
        sa_ref[...] = jnp.zeros_like(sa_ref)
        ys_ref[...] = jnp.zeros_like(ys_ref)

    cum = jnp.zeros((nk, nh), F32)
    for t in range(tb):
        cum = cum + e_ref[t].astype(F32)
        grow, shrink = jnp.exp(cum), jnp.exp(-cum)
        bs_ref[t] = b_ref[t].astype(F32) * grow
        ks_ref[t] = k_ref[t].astype(F32) * grow
        a_next = an_ref[t + 1] if t + 1 < tb else ann_ref[0]
        w_next = wr_ref[t + 1] if t + 1 < tb else wrn_ref[0]
        as_ref[t] = a_next.astype(F32) * shrink
        ws_ref[t] = w_next.astype(F32) * shrink
    pe_ref[...] = jnp.exp(-cum)

    def step(t, ln, last):
        sa, ys, v = sa_ref[:, ln], ys_ref[:, ln], v_ref[t, :, ln].astype(F32)
        y = ys + sa * sc_ref[t, 0:1, ln] + v * sc_ref[t, 1:2, ln]
        yc = y - jnp.sum(y, axis=0, keepdims=True) * (1.0 / nv)
        var = jnp.sum(yc * yc, axis=0, keepdims=True) * (1.0 / nv)
        y_ref[t, :, ln] = (yc * lax.rsqrt(var + RWKV_GN_EPS)).astype(y_ref.dtype)
        sa_n = jnp.zeros_like(sa)
        ys_n = jnp.zeros_like(sa)
        for j in range(nk):
            s = s_ref[j, :, ln] + sa * bs_ref[t, j:j + 1, ln] + v * ks_ref[t, j:j + 1, ln]
            s_ref[j, :, ln] = s * pe_ref[j:j + 1, ln] if last else s
            sa_n = sa_n + s * as_ref[t, j:j + 1, ln]
            ys_n = ys_n + s * ws_ref[t, j:j + 1, ln]
        sa_ref[:, ln] = sa_n
        ys_ref[:, ln] = ys_n

    for hb in range(nh // LANES):
        ln = slice(hb * LANES, (hb + 1) * LANES)

        def body(t, carry, ln=ln):
            step(t, ln, False)
            return carry

        lax.fori_loop(0, tb - 1, body, 0)
        step(tb - 1, ln, True)


def _rwkv_scan(e, an, bb, kk, wr, v, sc, *, tb=32, name):
    t, n, nh = e.shape
    assert t % tb == 0 and nh % LANES == 0
    blk = pl.BlockSpec((tb, n, nh), lambda i: (i, 0, 0))
    row = pl.BlockSpec((tb, 2, nh), lambda i: (i, 0, 0))
    nxt = pl.BlockSpec((1, n, nh), lambda i: (jnp.minimum((i + 1) * tb, t - 1), 0, 0))
    step_rows = pltpu.VMEM((tb, n, nh), F32)
    return pl.pallas_call(
        _rwkv_scan_kernel,
        out_shape=jax.ShapeDtypeStruct((t, n, nh), BF16),
        grid=(t // tb,),
        in_specs=[blk, blk, blk, blk, row, blk, blk, nxt, nxt],
        out_specs=blk,
        scratch_shapes=[pltpu.VMEM((n, n, nh), F32), pltpu.VMEM((n, nh), F32), pltpu.VMEM((n, nh), F32),
                        step_rows, step_rows, step_rows, step_rows, pltpu.VMEM((n, nh), F32)],
        compiler_params=_cparams("arbitrary"),
        name=name,
    )(e, bb, kk, v, sc, an, wr, an, wr)


def _rwkv_post_kernel(y_ref, bv_ref, g_ref, x_ref, gg_ref, gb_ref, w_ref, o_ref):
    z = ((y_ref[...].astype(F32) * gg_ref[...] + gb_ref[...] + bv_ref[...].astype(F32))
         * g_ref[...].astype(F32))
    o_ref[...] = x_ref[...] + _dot(z.astype(BF16), w_ref[...])


def _rwkv_post(y, bv, g, x2d, gn_g, gn_b, w_o, *, tm=1024, name):
    m, d = x2d.shape
    tile = pl.BlockSpec((tm, d), lambda i: (i, 0))
    vec = pl.BlockSpec((1, d), lambda i: (0, 0))
    return pl.pallas_call(
        _rwkv_post_kernel,
        out_shape=jax.ShapeDtypeStruct((m, d), F32),
        grid=(m // tm,),
        in_specs=[tile, tile, tile, tile, vec, vec, pl.BlockSpec((d, d), lambda i: (0, 0))],
        out_specs=tile,
        compiler_params=_cparams("parallel"),
        name=name,
    )(y, bv, g, x2d, _row2(gn_g), _row2(gn_b), w_o)


def _rwkv_mixer(x, g, p):
    b, t, d = x.shape
    n = RWKV_HEAD_DIM
    h = d // n
    lane = jnp.arange(LANES)[None, :]
    head = (jnp.arange(d) // n)[:, None]
    e = (head == lane).astype(BF16)
    e2 = (head + h == lane).astype(BF16)
    et = e.T
    el, an, bb, kk, wr, v, bv, gate, s = _rwkv_prep(x, g, p, e, e2, et, name="rwkv_prep")

    def tm(a):
        return a.reshape(b, t, h, n).transpose(1, 3, 0, 2).reshape(t, n, b * h)

    sc = s[:, :, :2 * h].reshape(b, t, 2, h).transpose(1, 2, 0, 3).reshape(t, 2, b * h)
    y = _rwkv_scan(tm(el), tm(an), tm(bb), tm(kk), tm(wr), tm(v), sc, name="rwkv_scan")
    y = y.reshape(t, n, b, h).transpose(2, 0, 3, 1).reshape(b * t, d)
    out = _rwkv_post(y, bv.reshape(b * t, d), gate.reshape(b * t, d), x.reshape(b * t, d),
                     p["gn_g"], p["gn_b"], p["w_o"], name="rwkv_post")
    return out.reshape(b, t, d)


def _sgu_tail_kernel(u_ref, v_ref, lg_ref, lb_ref, ws_ref, bst_ref, w_ref, x_ref, o_ref, vn_ref, gt_ref):
    tm, width = u_ref.shape[1], u_ref.shape[2]
    c = SGU_CHUNK
    gd = width // SGU_GROUPS
    vn_ref[...] = _layernorm(v_ref[0].astype(F32), lg_ref[...], lb_ref[...], LN_EPS).astype(BF16)
    tri = lax.broadcasted_iota(jnp.int32, (c, c), 0) >= lax.broadcasted_iota(jnp.int32, (c, c), 1)
    for gi in range(SGU_GROUPS):
        wsm = jnp.where(tri, ws_ref[gi], 0.0).astype(BF16)
        bias = bst_ref[:, gi:gi + 1]
        cols = slice(gi * gd, (gi + 1) * gd)
        for ci in range(tm // c):
            rows = slice(ci * c, (ci + 1) * c)
            mixed = _dot(wsm, vn_ref[rows, cols]) + bias
            gt_ref[rows, cols] = (u_ref[0, rows, cols].astype(F32) * mixed).astype(BF16)
    o_ref[0] = x_ref[0] + _dot(gt_ref[...], w_ref[...])


def _sgu_tail(uv, ln_g, ln_b, ws, bs, w_out, x, *, tm=512, name):
    b, t, d = x.shape
    width = uv.shape[-1] // 2
    assert t % tm == 0 and tm % SGU_CHUNK == 0
    return pl.pallas_call(
        _sgu_tail_kernel,
        out_shape=jax.ShapeDtypeStruct((b, t, d), F32),
        grid=(b, t // tm),
        in_specs=[pl.BlockSpec((1, tm, width), lambda bi, i: (bi, i, 0)),
                  pl.BlockSpec((1, tm, width), lambda bi, i: (bi, i, 1)),
                  pl.BlockSpec((1, width), lambda bi, i: (0, 0)),
                  pl.BlockSpec((1, width), lambda bi, i: (0, 0)),
                  pl.BlockSpec(ws.shape, lambda bi, i: (0, 0, 0)),
                  pl.BlockSpec((SGU_CHUNK, SGU_GROUPS), lambda bi, i: (0, 0)),
                  pl.BlockSpec((width, d), lambda bi, i: (0, 0)),
                  pl.BlockSpec((1, tm, d), lambda bi, i: (bi, i, 0))],
        out_specs=pl.BlockSpec((1, tm, d), lambda bi, i: (bi, i, 0)),
        scratch_shapes=[pltpu.VMEM((tm, width), BF16), pltpu.VMEM((tm, width), BF16)],
        compiler_params=_cparams("parallel", "parallel"),
        name=name,
    )(uv, uv, _row2(ln_g), _row2(ln_b), ws.astype(F32), bs.T.astype(F32), w_out, x)


def _ret_core_kernel(q_ref, k_ref, v_ref, g_ref, sin_ref, cos_ref, rot_ref, dm_ref, qd_ref, kd_ref, cd_ref,
                     o_ref, r_ref):
    @pl.when(pl.program_id(2) == 0)
    def _():
        r_ref[...] = jnp.zeros_like(r_ref)

    sin, cos, rot = sin_ref[...], cos_ref[...], rot_ref[...]
    dk, dv = r_ref.shape[1], r_ref.shape[2]
    for hl in range(r_ref.shape[0]):
        qb = q_ref[0, :, hl * dk:(hl + 1) * dk]
        kb = k_ref[0, :, hl * dk:(hl + 1) * dk]
        vb = v_ref[0, :, hl * dv:(hl + 1) * dv]
        q = qb.astype(F32) * cos + _dot(qb, rot) * sin
        k = (kb.astype(F32) * cos + _dot(kb, rot) * sin) * (dk ** -0.5)
        half = qb.shape[0] // 2
        q16, k16 = q.astype(BF16), k.astype(BF16)
        s_ee = (_dot_nt(q16[:half], k16[:half]) * dm_ref[hl, :half, :half]).astype(BF16)
        s_le = (_dot_nt(q16[half:], k16[:half]) * dm_ref[hl, half:, :half]).astype(BF16)
        s_ll = (_dot_nt(q16[half:], k16[half:]) * dm_ref[hl, half:, half:]).astype(BF16)
        inner = jnp.concatenate([_dot(s_ee, vb[:half]), _dot(s_le, vb[:half]) + _dot(s_ll, vb[half:])], axis=0)
        r = r_ref[hl]
        cross = _dot((q * qd_ref[hl]).astype(BF16), r.astype(BF16))
        ks = (k * kd_ref[hl]).astype(BF16)
        kv = lax.dot_general(ks, vb, (((0,), (0,)), ((), ())), preferred_element_type=F32)
        r_ref[hl] = r * cd_ref[hl, 0:1, 0:1] + kv
        o = inner + cross
        o = o * lax.rsqrt(jnp.mean(o * o, axis=-1, keepdims=True) + NORM_EPS)
        gate = g_ref[0, :, hl * dv:(hl + 1) * dv].astype(F32)
        o_ref[0, :, hl * dv:(hl + 1) * dv] = (gate * jax.nn.sigmoid(gate) * o).astype(o_ref.dtype)


def _ret_core(qkvg, t, d, *, name):
    b = qkvg.shape[0]
    hh, c = RET_HEADS, RET_CHUNK
    assert t % c == 0
    dk = d // hh
    dv = 2 * dk
    angle = jnp.repeat(1.0 / (10000.0 ** jnp.linspace(0.0, 1.0, dk // 2, dtype=F32)), 2)
    theta = jnp.arange(t, dtype=F32)[:, None] * angle[None]
    log_gamma = jnp.log(1.0 - 2.0 ** (-5.0 - jnp.arange(hh, dtype=F32)))
    idx = jnp.arange(c, dtype=F32)
    diff = idx[:, None] - idx[None, :]
    dm = jnp.where(diff[None] >= 0, jnp.exp(log_gamma[:, None, None] * jnp.maximum(diff, 0.0)[None]), 0.0)
    qd = jnp.exp(log_gamma[:, None] * (idx + 1.0))[:, :, None]
    kd = jnp.exp(log_gamma[:, None] * (c - 1.0 - idx))[:, :, None]
    cd = jnp.broadcast_to(jnp.exp(log_gamma * c)[:, None, None], (hh, SUBLANES, LANES))
    src, dst = jnp.arange(dk)[:, None], jnp.arange(dk)[None, :]
    rot = (jnp.where((dst % 2 == 1) & (src == dst - 1), 1.0, 0.0)
           - jnp.where((dst % 2 == 0) & (src == dst + 1), 1.0, 0.0)).astype(BF16)
    hs = RET_HEADS_PER_STEP
    nq = d // (hs * dk)
    nv = 2 * d // (hs * dv)
    return pl.pallas_call(
        _ret_core_kernel,
        out_shape=jax.ShapeDtypeStruct((b, t, hh * dv), BF16),
        grid=(b, hh // hs, t // c),
        in_specs=[pl.BlockSpec((1, c, hs * dk), lambda bi, h, n: (bi, n, h)),
                  pl.BlockSpec((1, c, hs * dk), lambda bi, h, n: (bi, n, nq + h)),
                  pl.BlockSpec((1, c, hs * dv), lambda bi, h, n: (bi, n, nv + h)),
                  pl.BlockSpec((1, c, hs * dv), lambda bi, h, n: (bi, n, nv + hh // hs + h)),
                  pl.BlockSpec((c, dk), lambda bi, h, n: (n, 0)),
                  pl.BlockSpec((c, dk), lambda bi, h, n: (n, 0)),
                  pl.BlockSpec((dk, dk), lambda bi, h, n: (0, 0)),
                  pl.BlockSpec((hs, c, c), lambda bi, h, n: (h, 0, 0)),
                  pl.BlockSpec((hs, c, 1), lambda bi, h, n: (h, 0, 0)),
                  pl.BlockSpec((hs, c, 1), lambda bi, h, n: (h, 0, 0)),
                  pl.BlockSpec((hs, SUBLANES, LANES), lambda bi, h, n: (h, 0, 0))],
        out_specs=pl.BlockSpec((1, c, hs * dv), lambda bi, h, n: (bi, n, h)),
        scratch_shapes=[pltpu.VMEM((hs, dk, dv), F32)],
        compiler_params=_cparams("parallel", "parallel", "arbitrary"),
        name=name,
    )(qkvg, qkvg, qkvg, qkvg, jnp.sin(theta), jnp.cos(theta), rot, dm, qd, kd, cd)


def _pad_axis(a, axis, to):
    pad = [(0, 0)] * a.ndim
    pad[axis] = (0, to - a.shape[axis])
    return jnp.pad(a, pad)


def kernel(x, mem, mix_norm_g, xattn_norm_g, mem_norm_g, xattn_wq, xattn_wkv, xattn_wo, ffn_norm_g, ffn_w_gate_up, ffn_w_down, conv_w_in, conv_b_in, conv_dw, conv_dw_b, conv_ln_g, conv_ln_b, conv_w_out, conv_b_out, rwkv_mix, rwkv_w_rkv, rwkv_w0, rwkv_w1, rwkv_w2, rwkv_a0, rwkv_a1, rwkv_a2, rwkv_g1, rwkv_g2, rwkv_k_k, rwkv_k_a, rwkv_r_k, rwkv_gn_g, rwkv_gn_b, rwkv_w_o, sgu_w_in, sgu_b_in, sgu_ln_g, sgu_ln_b, sgu_ws, sgu_bs, sgu_w_out, ret_w_in, ret_w_out, final_norm_g):
    b, t, d = x.shape
    n_mem = mem.shape[1]
    depth = mix_norm_g.shape[0]
    m = b * t
    mem2d = mem.reshape(b * n_mem, d)

    for i in range(depth):
        mixer, j = i % 4, i // 4
        g = mix_norm_g[i]
        if mixer == 0:
            z = _norm_mm(x.reshape(m, d), g, conv_w_in[j].astype(BF16), conv_b_in[j], glu=True,
                         name="conv_in")
            x = _conv_tail(z.reshape(b, t, d), conv_dw[j], conv_dw_b[j], conv_ln_g[j], conv_ln_b[j],
                           conv_w_out[j].astype(BF16), conv_b_out[j], x, name="conv_tail")
        elif mixer == 1:
            lw = _pad_axis(rwkv_w1[j], 1, LANES)
            la = _pad_axis(rwkv_a1[j], 1, LANES)
            lg = _pad_axis(rwkv_g1[j], 1, 2 * LANES)
            p = dict(mix=rwkv_mix[j].astype(F32),
                     wr=rwkv_w_rkv[j, 0].astype(BF16), wk=rwkv_w_rkv[j, 1].astype(BF16),
                     wv=rwkv_w_rkv[j, 2].astype(BF16),
                     w0=_row2(rwkv_w0[j]), w1=lw.astype(BF16),
                     w2=_pad_axis(rwkv_w2[j], 0, LANES).astype(BF16),
                     a0=_row2(rwkv_a0[j]), a1=la.astype(BF16),
                     a2=_pad_axis(rwkv_a2[j], 0, LANES).astype(BF16),
                     g1=lg.astype(BF16), g2=_pad_axis(rwkv_g2[j], 0, 2 * LANES).astype(BF16),
                     k_k=_row2(rwkv_k_k[j]), k_a=_row2(rwkv_k_a[j]), r_k=_row2(rwkv_r_k[j]),
                     gn_g=rwkv_gn_g[j], gn_b=rwkv_gn_b[j], w_o=rwkv_w_o[j].astype(BF16))
            x = _rwkv_mixer(x, g, p)
        elif mixer == 2:
            uv = _norm_mm(x.reshape(m, d), g, sgu_w_in[j].astype(BF16), sgu_b_in[j], act="gelu",
                          name="sgu_in")
            x = _sgu_tail(uv.reshape(b, t, -1), sgu_ln_g[j], sgu_ln_b[j], sgu_ws[j], sgu_bs[j],
                          sgu_w_out[j].astype(BF16), x, name="sgu_tail")
        else:
            qkvg = _norm_mm(x.reshape(m, d), g, ret_w_in[j].astype(BF16), name="ret_in")
            o = _ret_core(qkvg.reshape(b, t, -1), t, d, name="ret_core")
            x = _mm_res(o.reshape(m, -1), ret_w_out[j].astype(BF16), x.reshape(m, d),
                        name="ret_out").reshape(b, t, d)

        kv = _norm_mm(mem2d, mem_norm_g[i], xattn_wkv[i].astype(BF16), name="xattn_kv")
        x = _xattn(x, xattn_norm_g[i], xattn_wq[i].astype(BF16), kv.reshape(b, n_mem, 2 * d),
                   xattn_wo[i].astype(BF16), name="xattn")
        fg = final_norm_g if i == depth - 1 else None
        x = _swiglu(x.reshape(m, d), ffn_norm_g[i], ffn_w_gate_up[i].astype(BF16),
                    ffn_w_down[i].astype(BF16), fg, name="swiglu").reshape(b, t, d)
    return x
```
